```python
import jax, jax.numpy as jnp
from jax import lax
import numpy as np

D_MODEL = 1024
BATCH = 32
SEQ = 2048
DEPTH = 2
DEC_BATCH = 16
DEC_SEQ = 2048
PAST_LEN = 128

GRID_W = 64
ROPE_THETA = 10000.0
EPS = 1e-6
NEG = -1e30
Q_BLOCK = 128
HEAD_DIM = 64
N_BRANCH = 4
BRANCH_W = 256
A_HEADS = 4
A_NOPE = 64
A_ROPE = 32
A_V = 64
A_Q_LORA = 256
A_KV_LORA = 128
A_IN = A_Q_LORA + A_KV_LORA + A_ROPE
B_HEADS = 4
B_KV = 2
B_IN = (B_HEADS + 2 * B_KV) * HEAD_DIM
C_HEADS = 4
C_KV = 2
C_WINDOW = 128
C_BLOCK = 128
C_IN = (C_HEADS + 2 * C_KV) * HEAD_DIM
D_HEADS = 4
D_KV = 2
D_GROUPS = ((128, 1), (512, 4), (2048, 16))
D_BLOCK = 64
D_GROUP_IN = (D_HEADS + 2 * D_KV) * HEAD_DIM
D_IN = len(D_GROUPS) * D_GROUP_IN
GATE_IN = N_BRANCH * BRANCH_W
MERGE_IN = N_BRANCH * D_MODEL
N_IN = A_IN + B_IN + C_IN + D_IN + GATE_IN + MERGE_IN

kernel_name = "hybrid_gated_parallel_encoder"


def _split_cols(a, sizes):
    idx, acc = [], 0
    for s in sizes[:-1]:
        acc += s
        idx.append(acc)
    return jnp.split(a, idx, axis=-1)


def _rms_norm(x, g):
    xf = x.astype(jnp.float32)
    y = xf * lax.rsqrt(jnp.mean(xf * xf, axis=-1, keepdims=True) + EPS)
    return (y * g.astype(jnp.float32)).astype(x.dtype)


def _rope(x, pos):
    d = x.shape[-1]
    half = d // 2
    freqs = ROPE_THETA ** (-jnp.arange(half, dtype=jnp.float32) * 2.0 / d)
    ang = pos.astype(jnp.float32)[:, None] * freqs[None, :]
    ang = ang.reshape((ang.shape[0],) + (1,) * (x.ndim - 3) + (half,))
    cos, sin = jnp.cos(ang), jnp.sin(ang)
    xf = x.astype(jnp.float32)
    x1, x2 = xf[..., :half], xf[..., half:]
    return jnp.concatenate([x1 * cos - x2 * sin, x1 * sin + x2 * cos], axis=-1).astype(x.dtype)


def _axial_rope(x, rows, cols):
    half = x.shape[-1] // 2
    return jnp.concatenate([_rope(x[..., :half], rows), _rope(x[..., half:], cols)], axis=-1)


def _dense_attention(q, k, v, scale):
    bn, t = q.shape[:2]
    nb = t // Q_BLOCK
    qb = jnp.moveaxis(q.reshape((bn, nb, Q_BLOCK) + q.shape[2:]), 1, 0)

    def one(qblk):
        s = jnp.einsum('bqhgd,bkhd->bhgqk', qblk, k, preferred_element_type=jnp.float32) * scale
        p = jax.nn.softmax(s, axis=-1)
        return jnp.einsum('bhgqk,bkhd->bqhgd', p.astype(v.dtype), v)

    o = lax.map(one, qb)
    return jnp.moveaxis(o, 0, 1).reshape((bn, t) + o.shape[3:])


def _banded_attention(q, k, v, window, block, scale):
    bn, length = q.shape[:2]
    nb = -(-length // block)
    lp = nb * block
    qp = jnp.pad(q, ((0, 0), (0, lp - length)) + ((0, 0),) * (q.ndim - 2))
    kpad = ((0, 0), (block, lp - length + block), (0, 0), (0, 0))
    kb = jnp.pad(k, kpad).reshape((bn, nb + 2, block) + k.shape[2:])
    vb = jnp.pad(v, kpad).reshape((bn, nb + 2, block) + v.shape[2:])
    kw = jnp.concatenate([kb[:, :-2], kb[:, 1:-1], kb[:, 2:]], axis=2)
    vw = jnp.concatenate([vb[:, :-2], vb[:, 1:-1], vb[:, 2:]], axis=2)
    qb = qp.reshape((bn, nb, block) + q.shape[2:])

    def one(args):
        qblk, kblk, vblk, i = args
        qpos = i * block + jnp.arange(block)
        kpos = (i - 1) * block + jnp.arange(3 * block)
        valid = (jnp.abs(qpos[:, None] - kpos[None, :]) <= window) & (kpos[None, :] >= 0) & (kpos[None, :] < length)
        s = jnp.einsum('bqhgd,bkhd->bhgqk', qblk, kblk, preferred_element_type=jnp.float32) * scale
        s = jnp.where(valid, s, NEG)
        lse = jax.nn.logsumexp(s, axis=-1)
        p = jnp.exp(s - lse[..., None])
        o = jnp.einsum('bhgqk,bkhd->bqhgd', p.astype(vblk.dtype), vblk)
        return o, jnp.moveaxis(lse, 3, 1)

    o, lse = lax.map(one, (jnp.moveaxis(qb, 1, 0), jnp.moveaxis(kw, 1, 0), jnp.moveaxis(vw, 1, 0), jnp.arange(nb)))
    o = jnp.moveaxis(o, 0, 1).reshape((bn, lp) + o.shape[3:])[:, :length]
    lse = jnp.moveaxis(lse, 0, 1).reshape((bn, lp) + lse.shape[3:])[:, :length]
    return o, lse


def _mla_mixer(h, pos, q_a_norm, w_q_up, kv_a_norm, w_kv_up):
    bn, t = h.shape[:2]
    q_lat, kv_lat, k_rope = _split_cols(h, [A_Q_LORA, A_KV_LORA, A_ROPE])
    q = (_rms_norm(q_lat, q_a_norm) @ w_q_up).reshape(bn, t, A_HEADS, A_NOPE + A_ROPE)
    kv = (_rms_norm(kv_lat, kv_a_norm) @ w_kv_up).reshape(bn, t, A_HEADS, A_NOPE + A_V)
    q = jnp.concatenate([q[..., :A_NOPE], _rope(q[..., A_NOPE:], pos)], axis=-1)
    k_r = _rope(k_rope[:, :, None, :], pos)
    k = jnp.concatenate([kv[..., :A_NOPE], jnp.broadcast_to(k_r, (bn, t, A_HEADS, A_ROPE))], axis=-1)
    v = kv[..., A_NOPE:]
    o = _dense_attention(q[:, :, :, None, :], k, v, (A_NOPE + A_ROPE) ** -0.5)
    return o.reshape(bn, t, A_HEADS * A_V)


def _axial_gqa_mixer(h, rows, cols, q_norm, k_norm):
    bn, t = h.shape[:2]
    q, k, v = _split_cols(h, [B_HEADS * HEAD_DIM, B_KV * HEAD_DIM, B_KV * HEAD_DIM])
    q = _axial_rope(_rms_norm(q.reshape(bn, t, B_HEADS, HEAD_DIM), q_norm), rows, cols)
    k = _axial_rope(_rms_norm(k.reshape(bn, t, B_KV, HEAD_DIM), k_norm), rows, cols)
    q = q.reshape(bn, t, B_KV, B_HEADS // B_KV, HEAD_DIM)
    v = v.reshape(bn, t, B_KV, HEAD_DIM)
    o = _dense_attention(q, k, v, HEAD_DIM ** -0.5)
    return o.reshape(bn, t, B_HEADS * HEAD_DIM)


def _sink_window_mixer(h, pos, sink):
    bn, t = h.shape[:2]
    g = C_HEADS // C_KV
    q, k, v = _split_cols(h, [C_HEADS * HEAD_DIM, C_KV * HEAD_DIM, C_KV * HEAD_DIM])
    q = _rope(q.reshape(bn, t, C_HEADS, HEAD_DIM), pos).reshape(bn, t, C_KV, g, HEAD_DIM)
    k = _rope(k.reshape(bn, t, C_KV, HEAD_DIM), pos)
    v = v.reshape(bn, t, C_KV, HEAD_DIM)
    o, lse = _banded_attention(q, k, v, C_WINDOW, C_BLOCK, HEAD_DIM ** -0.5)
    lse_tot = jnp.logaddexp(lse, sink.astype(jnp.float32).reshape(C_KV, g))
    o = o * jnp.exp(lse - lse_tot)[..., None].astype(o.dtype)
    return o.reshape(bn, t, C_HEADS * HEAD_DIM)


def _dilated_mixer(h, pos):
    bn, t = h.shape[:2]
    g = D_HEADS // D_KV
    outs, lses = [], []
    for gi, (window, dil) in enumerate(D_GROUPS):
        hg = h[..., gi * D_GROUP_IN:(gi + 1) * D_GROUP_IN]
        q, k, v = _split_cols(hg, [D_HEADS * HEAD_DIM, D_KV * HEAD_DIM, D_KV * HEAD_DIM])
        q = _rope(q.reshape(bn, t, D_HEADS, HEAD_DIM), pos).reshape(bn, t, D_KV, g, HEAD_DIM)
        k = _rope(k.reshape(bn, t, D_KV, HEAD_DIM), pos)
        v = v.reshape(bn, t, D_KV, HEAD_DIM)

        def to_strided(a, dil=dil):
            rest = a.shape[2:]
            a = a.reshape((bn, t // dil, dil) + rest)
            return jnp.moveaxis(a, 2, 1).reshape((bn * dil, t // dil) + rest)

        def from_strided(a, dil=dil):
            rest = a.shape[2:]
            a = a.reshape((bn, dil, t // dil) + rest)
            return jnp.moveaxis(a, 1, 2).reshape((bn, t) + rest)

        o, lse = _banded_attention(to_strided(q), to_strided(k), to_strided(v), window // (2 * dil), D_BLOCK, HEAD_DIM ** -0.5)
        outs.append(from_strided(o))
        lses.append(from_strided(lse))
    w = jax.nn.softmax(jnp.stack(lses, axis=0), axis=0)
    o = jnp.einsum('nbthg,nbthgd->bthgd', w.astype(outs[0].dtype), jnp.stack(outs, axis=0))
    return o.reshape(bn, t, D_HEADS * HEAD_DIM)


def _layer(x, pos, rows, cols, norm_g, w_in, a_q_norm, w_q_up, a_kv_norm, w_kv_up,
           b_q_norm, b_k_norm, c_sink, w_branch, w_out):
    bn, t = x.shape[:2]
    xn = _rms_norm(x, norm_g)
    h = xn @ w_in
    h_a, h_b, h_c, h_d, z, mg = _split_cols(h, [A_IN, B_IN, C_IN, D_IN, GATE_IN, MERGE_IN])
    ys = [
        _mla_mixer(h_a, pos, a_q_norm, w_q_up, a_kv_norm, w_kv_up),
        _axial_gqa_mixer(h_b, rows, cols, b_q_norm, b_k_norm),
        _sink_window_mixer(h_c, pos, c_sink),
        _dilated_mixer(h_d, pos),
    ]
    merged = jnp.zeros_like(x)
    for i in range(N_BRANCH):
        yi = ys[i] * jax.nn.silu(z[..., i * BRANCH_W:(i + 1) * BRANCH_W])
        gi = jax.nn.sigmoid(mg[..., i * D_MODEL:(i + 1) * D_MODEL])
        merged = merged + gi * (yi @ w_branch[i])
    return x + merged @ w_out


def _trunk(x, norm_in, w_in, a_q_norm, w_q_up, a_kv_norm, w_kv_up,
           b_q_norm, b_k_norm, c_sink, w_branch, w_out, final_norm):
    t = x.shape[1]
    rows_n = t // GRID_W
    pos = jnp.arange(t)
    rows = jnp.repeat(jnp.arange(rows_n), GRID_W)
    cols = jnp.tile(jnp.arange(GRID_W), rows_n)
    for l in range(DEPTH):
        x = _layer(x, pos, rows, cols, norm_in[l], w_in[l], a_q_norm[l], w_q_up[l], a_kv_norm[l], w_kv_up[l],
                   b_q_norm[l], b_k_norm[l], c_sink[l], w_branch[l], w_out[l])
    return _rms_norm(x, final_norm)


def setup_inputs(seed: int = 0) -> dict:
    key = jax.random.key(seed)
    ks = jax.random.split(key, 16)
    f = jnp.float32

    def gain(k, shape):
        return jnp.ones(shape, f) + 0.02 * jax.random.normal(k, shape, f)

    return {
        'x_prompt': jax.random.normal(ks[0], (BATCH, SEQ, D_MODEL), f),
        'x_sample': jax.random.normal(ks[1], (DEC_BATCH, DEC_SEQ, D_MODEL), f),
        'norm_in': gain(ks[2], (DEPTH, D_MODEL)),
        'w_in': jax.random.normal(ks[3], (DEPTH, D_MODEL, N_IN), f) * D_MODEL ** -0.5,
        'a_q_norm': gain(ks[4], (DEPTH, A_Q_LORA)),
        'w_q_up': jax.random.normal(ks[5], (DEPTH, A_Q_LORA, A_HEADS * (A_NOPE + A_ROPE)), f) * A_Q_LORA ** -0.5,
        'a_kv_norm': gain(ks[6], (DEPTH, A_KV_LORA)),
        'w_kv_up': jax.random.normal(ks[7], (DEPTH, A_KV_LORA, A_HEADS * (A_NOPE + A_V)), f) * A_KV_LORA ** -0.5,
        'b_q_norm': gain(ks[8], (DEPTH, HEAD_DIM)),
        'b_k_norm': gain(ks[9], (DEPTH, HEAD_DIM)),
        'c_sink': 0.5 * jax.random.normal(ks[10], (DEPTH, C_HEADS), f),
        'w_branch': jax.random.normal(ks[11], (DEPTH, N_BRANCH, BRANCH_W, D_MODEL), f) * BRANCH_W ** -0.5,
        'w_out': jax.random.normal(ks[12], (DEPTH, D_MODEL, D_MODEL), f) * D_MODEL ** -0.5,
        'final_norm': gain(ks[13], (D_MODEL,)),
    }


def reference(x_prompt, x_sample, norm_in, w_in, a_q_norm, w_q_up, a_kv_norm, w_kv_up,
              b_q_norm, b_k_norm, c_sink, w_branch, w_out, final_norm):
    y_prompt = _trunk(x_prompt, norm_in, w_in, a_q_norm, w_q_up, a_kv_norm, w_kv_up,
                      b_q_norm, b_k_norm, c_sink, w_branch, w_out, final_norm)
    y_sample = _trunk(x_sample, norm_in, w_in, a_q_norm, w_q_up, a_kv_norm, w_kv_up,
                      b_q_norm, b_k_norm, c_sink, w_branch, w_out, final_norm)
    return (y_prompt, y_sample)
```

```python
import functools

import jax
import jax.numpy as jnp
from jax import lax
from jax.experimental import pallas as pl
from jax.experimental.pallas import tpu as pltpu

D_MODEL = 1024
SEQ = 2048
DEPTH = 2
GRID_W = 64
ROPE_THETA = 10000.0
EPS = 1e-6
NEG = -1e30
HEAD_DIM = 64
LANES = 128
BRANCH_W = 256
N_BRANCH = 4
A_HEADS = 4
A_NOPE = 64
A_ROPE = 32
A_V = 64
A_Q_LORA = 256
A_KV_LORA = 128
A_IN = A_Q_LORA + A_KV_LORA + A_ROPE
QKV_W = 512
C_WINDOW = 128
D_GROUPS = ((128, 1), (512, 4), (2048, 16))
D_BAND = 64
GATE_OFF = A_IN + 2 * QKV_W + len(D_GROUPS) * QKV_W
MERGE_OFF = GATE_OFF + N_BRANCH * BRANCH_W
A_SCALE = (A_NOPE + A_ROPE) ** -0.5
QK_SCALE = HEAD_DIM ** -0.5

TM = 512
TQ = 256
VMEM_LIMIT = 56 * 1024 * 1024

F32 = jnp.float32
BF16 = jnp.bfloat16


def _dot(a, b):
    return jnp.dot(a, b, preferred_element_type=F32)


def _dot_nt(a, b):
    return lax.dot_general(a, b, (((1,), (1,)), ((), ())), preferred_element_type=F32)


def _rms(x, g):
    return x * lax.rsqrt(jnp.mean(x * x, axis=-1, keepdims=True) + EPS) * g


def _sigmoid(x):
    return 1.0 / (1.0 + jnp.exp(-x))


def _rope(a, cos, sin, shift):
    lane = lax.broadcasted_iota(jnp.int32, (1, LANES), 1)
    first = (lane % (2 * shift)) < shift
    outs = []
    for j in range(a.shape[1] // LANES):
        s = a[:, j * LANES:(j + 1) * LANES]
        partner = jnp.where(first, pltpu.roll(s, LANES - shift, 1), pltpu.roll(s, shift, 1))
        outs.append(s * cos + partner * sin)
    return outs[0] if len(outs) == 1 else jnp.concatenate(outs, axis=1)


def _head_mean_sq(x, gm):
    x2 = x * x
    hi = x2.astype(BF16)
    lo = (x2 - hi.astype(F32)).astype(BF16)
    return _dot(hi, gm) + _dot(lo, gm)


def _proj_body(x_ref, g_ref, w_ref, wq_ref, wkv_ref, gq_ref, gkv_ref, gbq_ref, gbk_ref, gm_ref, tab_ref,
               a_ref, b_ref, c_ref, d0_ref, d1_ref, d2_ref):
    xn = _rms(x_ref[...], g_ref[...]).astype(BF16)
    cos_a, sin_a = tab_ref[0], tab_ref[1]
    cos_b, sin_b = tab_ref[2], tab_ref[3]
    cos_c, sin_c = tab_ref[4], tab_ref[5]

    h = _dot(xn, w_ref[:, 0:512])
    q = _dot(_rms(h[:, 0:256], gq_ref[...]).astype(BF16), wq_ref[...])
    q = _rope(q, cos_a, sin_a, A_ROPE // 2) * A_SCALE
    kv = _dot(_rms(h[:, 256:384], gkv_ref[...]).astype(BF16), wkv_ref[...])
    kr = _rope(h[:, 384:512], cos_a, sin_a, A_ROPE // 2)
    k = kv[:, 0:512] + jnp.concatenate([kr] * A_HEADS, axis=1)
    a_ref[:, 0:512] = q.astype(BF16)
    a_ref[:, 512:1024] = k.astype(BF16)
    a_ref[:, 1024:1280] = kv[:, 512:768].astype(BF16)

    h = _dot(xn, w_ref[:, 512:1024])
    q, k = h[:, 0:256], h[:, 256:384]
    q = q * lax.rsqrt(_head_mean_sq(q, gm_ref[...]) + EPS) * gbq_ref[...]
    k = k * lax.rsqrt(_head_mean_sq(k, gm_ref[0:128, 0:128]) + EPS) * gbk_ref[...]
    b_ref[:, 0:256] = (_rope(q, cos_b, sin_b, HEAD_DIM // 4) * QK_SCALE).astype(BF16)
    b_ref[:, 256:384] = _rope(k, cos_b, sin_b, HEAD_DIM // 4).astype(BF16)
    b_ref[:, 384:512] = h[:, 384:512].astype(BF16)

    for n, o_ref in enumerate((c_ref, d0_ref, d1_ref, d2_ref)):
        h = _dot(xn, w_ref[:, 1024 + n * QKV_W:1024 + (n + 1) * QKV_W])
        o_ref[:, 0:256] = (_rope(h[:, 0:256], cos_c, sin_c, HEAD_DIM // 2) * QK_SCALE).astype(BF16)
        o_ref[:, 256:384] = _rope(h[:, 256:384], cos_c, sin_c, HEAD_DIM // 2).astype(BF16)
        o_ref[:, 384:512] = h[:, 384:512].astype(BF16)


def _const_spec(shape):
    return pl.BlockSpec(shape, lambda i: (0,) * len(shape), pipeline_mode=pl.Buffered(1))


def _project(x, lw, tables):
    n = x.shape[0]
    nt = SEQ // TM
    row = lambda w: pl.BlockSpec((TM, w), lambda i: (i, 0))
    out_w = (1280, QKV_W, QKV_W, QKV_W, QKV_W, QKV_W)
    return pl.pallas_call(
        _proj_body,
        grid=(n // TM,),
        in_specs=[
            row(D_MODEL),
            _const_spec((1, D_MODEL)),
            _const_spec((D_MODEL, 3072)),
            _const_spec((A_Q_LORA, 512)),
            _const_spec((A_KV_LORA, 768)),
            _const_spec((1, A_Q_LORA)),
            _const_spec((1, A_KV_LORA)),
            _const_spec((1, 256)),
            _const_spec((1, 128)),
            _const_spec((256, 256)),
            pl.BlockSpec((6, TM, LANES), lambda i: (0, i % nt, 0)),
        ],
        out_specs=[row(w) for w in out_w],
        out_shape=[jax.ShapeDtypeStruct((n, w), BF16) for w in out_w],
        compiler_params=pltpu.CompilerParams(dimension_semantics=("parallel",), vmem_limit_bytes=VMEM_LIMIT),
        name="proj",
    )(x, lw["g_in"], lw["w_attn"], lw["wq"], lw["wkv"], lw["gq"], lw["gkv"], lw["gbq"], lw["gbk"], lw["gm"], tables)


def _lane_lo():
    return lax.broadcasted_iota(jnp.int32, (1, LANES), 1) < HEAD_DIM


def _attend(q, k, v, bias=None, sink=None):
    s = _dot_nt(q, k)
    if bias is not None:
        s = s + bias
    m = jnp.max(s, axis=-1, keepdims=True)
    p = jnp.exp(s - m)
    l = jnp.sum(p, axis=-1, keepdims=True)
    if sink is not None:
        l = l + jnp.exp(sink - m)
    return _dot(p.astype(BF16), v), m, l


def _dense_body(q_ref, k_ref, v_ref, o_ref, *, shared):
    lo = _lane_lo()
    zero = jnp.zeros((), BF16)

    def tile(i, carry):
        r = pl.multiple_of(i * TQ, TQ)
        v = v_ref[0]
        v_a, v_b = jnp.where(lo, v, zero), jnp.where(lo, zero, v)
        if shared:
            qs = q_ref[0, pl.ds(r, TQ), :]
            q_a, q_b = jnp.where(lo, qs, zero), jnp.where(lo, zero, qs)
            k_a = k_b = k_ref[0]
        else:
            q_a, q_b = q_ref[0, pl.ds(r, TQ), 0:LANES], q_ref[0, pl.ds(r, TQ), LANES:2 * LANES]
            k_a, k_b = k_ref[0, :, 0:LANES], k_ref[0, :, LANES:2 * LANES]
        o_a, _, l_a = _attend(q_a, k_a, v_a)
        o_b, _, l_b = _attend(q_b, k_b, v_b)
        o_ref[0, pl.ds(r, TQ), :] = (o_a * (1.0 / l_a) + o_b * (1.0 / l_b)).astype(BF16)
        return carry

    lax.fori_loop(0, SEQ // TQ, tile, 0)


def _attn_params(n_grid):
    return pltpu.CompilerParams(dimension_semantics=("parallel",) * n_grid, vmem_limit_bytes=VMEM_LIMIT)


def _latent_attention(a):
    bn = a.shape[0]
    return pl.pallas_call(
        functools.partial(_dense_body, shared=False),
        grid=(bn, 2),
        in_specs=[
            pl.BlockSpec((1, SEQ, 256), lambda b, s: (b, 0, s)),
            pl.BlockSpec((1, SEQ, 256), lambda b, s: (b, 0, 2 + s)),
            pl.BlockSpec((1, SEQ, LANES), lambda b, s: (b, 0, 8 + s)),
        ],
        out_specs=pl.BlockSpec((1, SEQ, LANES), lambda b, s: (b, 0, s)),
        out_shape=jax.ShapeDtypeStruct((bn, SEQ, BRANCH_W), BF16),
        compiler_params=_attn_params(2),
        name="attn_latent",
    )(a, a, a)


def _axial_attention(qkv):
    bn = qkv.shape[0]
    return pl.pallas_call(
        functools.partial(_dense_body, shared=True),
        grid=(bn, 2),
        in_specs=[
            pl.BlockSpec((1, SEQ, LANES), lambda b, s: (b, 0, s)),
            pl.BlockSpec((1, SEQ, LANES), lambda b, s: (b, 0, 2)),
            pl.BlockSpec((1, SEQ, LANES), lambda b, s: (b, 0, 3)),
        ],
        out_specs=pl.BlockSpec((1, SEQ, LANES), lambda b, s: (b, 0, s)),
        out_shape=jax.ShapeDtypeStruct((bn, SEQ, BRANCH_W), BF16),
        compiler_params=_attn_params(2),
        name="attn_axial",
    )(qkv, qkv, qkv)


def _band_bias(q0, k0, tq, tk, band):
    qpos = q0 + lax.broadcasted_iota(jnp.int32, (tq, 1), 0)
    kpos = k0 + lax.broadcasted_iota(jnp.int32, (1, tk), 1)
    return jnp.where(jnp.abs(qpos - kpos) <= band, 0.0, NEG).astype(F32)


C_KEYS = TQ + 2 * C_WINDOW


def _window_body(sink_ref, q_ref, k_ref, v_ref, o_ref):
    lo = _lane_lo()
    zero = jnp.zeros((), BF16)
    slab = pl.program_id(1)
    sink_a, sink_b = sink_ref[slab], sink_ref[slab + 2]

    def tile(i, carry):
        r = pl.multiple_of(i * TQ, TQ)
        ws = pl.multiple_of(jnp.clip(r - C_WINDOW, 0, SEQ - C_KEYS), C_WINDOW)
        bias = _band_bias(r, ws, TQ, C_KEYS, C_WINDOW)
        qs = q_ref[0, pl.ds(r, TQ), :]
        k = k_ref[0, pl.ds(ws, C_KEYS), :]
        v = v_ref[0, pl.ds(ws, C_KEYS), :]
        o_a, _, l_a = _attend(jnp.where(lo, qs, zero), k, jnp.where(lo, v, zero), bias, sink_a)
        o_b, _, l_b = _attend(jnp.where(lo, zero, qs), k, jnp.where(lo, zero, v), bias, sink_b)
        o_ref[0, pl.ds(r, TQ), :] = (o_a * (1.0 / l_a) + o_b * (1.0 / l_b)).astype(BF16)
        return carry

    lax.fori_loop(0, SEQ // TQ, tile, 0)


def _window_attention(qkv, sink):
    bn = qkv.shape[0]
    return pl.pallas_call(
        _window_body,
        grid=(bn, 2),
        in_specs=[
            pl.BlockSpec(memory_space=pltpu.MemorySpace.SMEM),
            pl.BlockSpec((1, SEQ, LANES), lambda b, s: (b, 0, s)),
            pl.BlockSpec((1, SEQ, LANES), lambda b, s: (b, 0, 2)),
            pl.BlockSpec((1, SEQ, LANES), lambda b, s: (b, 0, 3)),
        ],
        out_specs=pl.BlockSpec((1, SEQ, LANES), lambda b, s: (b, 0, s)),
        out_shape=jax.ShapeDtypeStruct((bn, SEQ, BRANCH_W), BF16),
        compiler_params=_attn_params(2),
        name="attn_window",
    )(sink, qkv, qkv, qkv)


def _dilated_body(d0_ref, d1_ref, d2_ref, o_ref, acc_ref, lse_ref):
    lo = _lane_lo()
    zero = jnp.zeros((), BF16)

    def tile(g, ref, dil, c, r, tq, tk):
        length = SEQ // dil
        if isinstance(r, int):
            ws = min(max(r - D_BAND, 0), length - tk)
        else:
            ws = pl.multiple_of(jnp.clip(r - D_BAND, 0, length - tk), D_BAND)
        bias = _band_bias(r, ws, tq, tk, D_BAND)
        base = c * QKV_W
        k = ref[0, pl.ds(ws, tk), base + 256:base + 384]
        v = ref[0, pl.ds(ws, tk), base + 384:base + 512]
        v_a, v_b = jnp.where(lo, v, zero), jnp.where(lo, zero, v)
        rows = pl.ds(r * dil + c, tq, stride=dil) if dil > 1 else pl.ds(r, tq)
        for slab in range(2):
            qs = ref[0, pl.ds(r, tq), base + slab * LANES:base + (slab + 1) * LANES]
            o_a, m_a, l_a = _attend(jnp.where(lo, qs, zero), k, v_a, bias)
            o_b, m_b, l_b = _attend(jnp.where(lo, zero, qs), k, v_b, bias)
            acc_ref[g, slab, rows, :] = o_a * (1.0 / l_a) + o_b * (1.0 / l_b)
            lse_ref[g, slab, rows, :] = jnp.where(lo, m_a + jnp.log(l_a), m_b + jnp.log(l_b))

    def g0_tile(i, carry):
        tile(0, d0_ref, 1, 0, pl.multiple_of(i * TQ, TQ), TQ, TQ + 2 * D_BAND)
        return carry

    lax.fori_loop(0, SEQ // TQ, g0_tile, 0)
    for c in range(4):
        for r in (0, TQ):
            tile(1, d1_ref, 4, c, r, TQ, TQ + 2 * D_BAND)
    for c in range(16):
        tile(2, d2_ref, 16, c, 0, 128, 128)

    def merge(i, carry):
        rows = pl.ds(pl.multiple_of(i * TQ, TQ), TQ)
        for slab in range(2):
            l0, l1, l2 = lse_ref[0, slab, rows, :], lse_ref[1, slab, rows, :], lse_ref[2, slab, rows, :]
            m = jnp.maximum(jnp.maximum(l0, l1), l2)
            w0, w1, w2 = jnp.exp(l0 - m), jnp.exp(l1 - m), jnp.exp(l2 - m)
            num = w0 * acc_ref[0, slab, rows, :] + w1 * acc_ref[1, slab, rows, :] + w2 * acc_ref[2, slab, rows, :]
            o_ref[0, rows, slab * LANES:(slab + 1) * LANES] = (num * (1.0 / (w0 + w1 + w2))).astype(BF16)
        return carry

    lax.fori_loop(0, SEQ // TQ, merge, 0)


def _dilated_attention(d0, d1, d2):
    bn = d0.shape[0]
    views = [d.reshape(bn, SEQ // dil, dil * QKV_W) for d, (_, dil) in zip((d0, d1, d2), D_GROUPS)]
    return pl.pallas_call(
        _dilated_body,
        grid=(bn,),
        in_specs=[pl.BlockSpec((1,) + v.shape[1:], lambda b: (b, 0, 0), pipeline_mode=pl.Buffered(1)) for v in views],
        out_specs=pl.BlockSpec((1, SEQ, BRANCH_W), lambda b: (b, 0, 0)),
        out_shape=jax.ShapeDtypeStruct((bn, SEQ, BRANCH_W), BF16),
        scratch_shapes=[pltpu.VMEM((3, 2, SEQ, LANES), F32), pltpu.VMEM((3, 2, SEQ, LANES), F32)],
        compiler_params=_attn_params(1),
        name="attn_dilated",
    )(*views)


def _merge_body(x_ref, g_ref, ya_ref, yb_ref, yc_ref, yd_ref, wz_ref, wmg_ref, wb_ref, wo_ref, fg_ref, o_ref, *, final):
    x = x_ref[...]
    xn = _rms(x, g_ref[...]).astype(BF16)
    merged = jnp.zeros((TM, D_MODEL), F32)
    for i, y_ref in enumerate((ya_ref, yb_ref, yc_ref, yd_ref)):
        z = _dot(xn, wz_ref[:, i * BRANCH_W:(i + 1) * BRANCH_W])
        y = y_ref[...].astype(F32) * (z * _sigmoid(z))
        branch = _dot(y.astype(BF16), wb_ref[i])
        gate = _sigmoid(_dot(xn, wmg_ref[:, i * D_MODEL:(i + 1) * D_MODEL]))
        merged = merged + gate * branch
    out = x + _dot(merged.astype(BF16), wo_ref[...])
    if final:
        out = _rms(out, fg_ref[...])
    o_ref[...] = out


def _merge(x, ys, lw, final_g, final):
    n = x.shape[0]
    row = lambda w: pl.BlockSpec((TM, w), lambda i: (i, 0))
    return pl.pallas_call(
        functools.partial(_merge_body, final=final),
        grid=(n // TM,),
        in_specs=[
            row(D_MODEL),
            _const_spec((1, D_MODEL)),
            row(BRANCH_W), row(BRANCH_W), row(BRANCH_W), row(BRANCH_W),
            _const_spec((D_MODEL, N_BRANCH * BRANCH_W)),
            _const_spec((D_MODEL, N_BRANCH * D_MODEL)),
            _const_spec((N_BRANCH, BRANCH_W, D_MODEL)),
            _const_spec((D_MODEL, D_MODEL)),
            _const_spec((1, D_MODEL)),
        ],
        out_specs=row(D_MODEL),
        out_shape=jax.ShapeDtypeStruct((n, D_MODEL), F32),
        compiler_params=pltpu.CompilerParams(dimension_semantics=("parallel",), vmem_limit_bytes=VMEM_LIMIT),
        name="merge",
    )(x, lw["g_in"], *ys, lw["wz"], lw["wmg"], lw["wb"], lw["wo"], final_g)


def _gqa_cols(w):
    q = w[..., 0:256].reshape(w.shape[:-1] + (4, HEAD_DIM))[..., (0, 2, 1, 3), :].reshape(w.shape[:-1] + (256,))
    return jnp.concatenate([q, w[..., 256:512]], axis=-1)


def _slab_heads(w, axis):
    w = jnp.moveaxis(w, axis, -1)
    w = w.reshape(w.shape[:-1] + (4, HEAD_DIM))[..., (0, 2, 1, 3), :].reshape(w.shape[:-1] + (256,))
    return jnp.moveaxis(w, -1, axis)


def _layer_weights(l, norm_in, w_in, a_q_norm, w_q_up, a_kv_norm, w_kv_up, b_q_norm, b_k_norm, w_branch, w_out):
    w = w_in[l]
    zeros = lambda c: jnp.zeros((D_MODEL, c), F32)
    off_b = A_IN
    cols = [w[:, 0:384], zeros(64), w[:, 384:416], zeros(32)]
    for n in range(2 + len(D_GROUPS)):
        cols.append(_gqa_cols(w[:, off_b + n * QKV_W:off_b + (n + 1) * QKV_W]))
    w_attn = jnp.concatenate(cols, axis=1)

    wq = w_q_up[l].reshape(A_Q_LORA, A_HEADS, A_NOPE + A_ROPE)
    wq = jnp.pad(wq, ((0, 0), (0, 0), (0, LANES - A_NOPE - A_ROPE))).reshape(A_Q_LORA, A_HEADS * LANES)
    wkv = w_kv_up[l].reshape(A_KV_LORA, A_HEADS, A_NOPE + A_V)
    wk = jnp.pad(wkv[:, :, :A_NOPE], ((0, 0), (0, 0), (0, LANES - A_NOPE))).reshape(A_KV_LORA, A_HEADS * LANES)
    wv = wkv[:, :, A_NOPE:].reshape(A_KV_LORA, A_HEADS * A_V)

    wz = w[:, GATE_OFF:MERGE_OFF]
    wz = jnp.concatenate([wz[:, 0:256]] + [_slab_heads(wz[:, i * 256:(i + 1) * 256], 1) for i in (1, 2, 3)], axis=1)
    wb = jnp.stack([w_branch[l, 0]] + [_slab_heads(w_branch[l, i], 0) for i in (1, 2, 3)])

    head_id = jnp.arange(256) // HEAD_DIM
    return {
        "g_in": norm_in[l].reshape(1, D_MODEL),
        "w_attn": w_attn.astype(BF16),
        "wq": wq.astype(BF16),
        "wkv": jnp.concatenate([wk, wv], axis=1).astype(BF16),
        "gq": a_q_norm[l].reshape(1, A_Q_LORA),
        "gkv": a_kv_norm[l].reshape(1, A_KV_LORA),
        "gbq": jnp.tile(b_q_norm[l], 4).reshape(1, 256),
        "gbk": jnp.tile(b_k_norm[l], 2).reshape(1, 128),
        "gm": jnp.where(head_id[:, None] == head_id[None, :], 1.0 / HEAD_DIM, 0.0).astype(BF16),
        "wz": wz.astype(BF16),
        "wmg": w[:, MERGE_OFF:].astype(BF16),
        "wb": wb.astype(BF16),
        "wo": w_out[l].astype(BF16),
    }


def _rope_tables():
    pos = jnp.arange(SEQ)
    rows = jnp.repeat(jnp.arange(SEQ // GRID_W), GRID_W)
    cols = jnp.tile(jnp.arange(GRID_W), SEQ // GRID_W)

    def ang(p, d):
        freqs = ROPE_THETA ** (-jnp.arange(d // 2, dtype=F32) * 2.0 / d)
        return p.astype(F32)[:, None] * freqs[None, :]

    def pair(a):
        return jnp.concatenate([jnp.cos(a), jnp.cos(a)], axis=1), jnp.concatenate([-jnp.sin(a), jnp.sin(a)], axis=1)

    one, zero = jnp.ones((SEQ, 1), F32), jnp.zeros((SEQ, 1), F32)
    ca, sa = pair(ang(pos, A_ROPE))
    cos_a = jnp.concatenate([jnp.tile(one, (1, 64)), ca, jnp.tile(one, (1, 32))], axis=1)
    sin_a = jnp.concatenate([jnp.tile(zero, (1, 64)), sa, jnp.tile(zero, (1, 32))], axis=1)
    (cr, sr), (cc, sc) = pair(ang(rows, HEAD_DIM // 2)), pair(ang(cols, HEAD_DIM // 2))
    cos_b, sin_b = jnp.tile(jnp.concatenate([cr, cc], axis=1), (1, 2)), jnp.tile(jnp.concatenate([sr, sc], axis=1), (1, 2))
    cp, sp = pair(ang(pos, HEAD_DIM))
    return jnp.stack([cos_a, sin_a, cos_b, sin_b, jnp.tile(cp, (1, 2)), jnp.tile(sp, (1, 2))])


def _trunk(x, norm_in, w_in, a_q_norm, w_q_up, a_kv_norm, w_kv_up, b_q_norm, b_k_norm, c_sink, w_branch, w_out, final_norm):
    bn = x.shape[0]
    tables = _rope_tables()
    x = x.reshape(bn * SEQ, D_MODEL)
    final_g = final_norm.reshape(1, D_MODEL)
    for l in range(DEPTH):
        lw = _layer_weights(l, norm_in, w_in, a_q_norm, w_q_up, a_kv_norm, w_kv_up, b_q_norm, b_k_norm, w_branch, w_out)
        a, b, c, d0, d1, d2 = _project(x, lw, tables)
        seq = lambda t: t.reshape(bn, SEQ, t.shape[-1])
        ys = (
            _latent_attention(seq(a)),
            _axial_attention(seq(b)),
            _window_attention(seq(c), c_sink[l]),
            _dilated_attention(seq(d0), seq(d1), seq(d2)),
        )
        ys = [y.reshape(bn * SEQ, BRANCH_W) for y in ys]
        x = _merge(x, ys, lw, final_g, final=(l == DEPTH - 1))
    return x.reshape(bn, SEQ, D_MODEL)


def kernel(x_prompt, x_sample, norm_in, w_in, a_q_norm, w_q_up, a_kv_norm, w_kv_up, b_q_norm, b_k_norm, c_sink, w_branch, w_out, final_norm):
    nb = x_prompt.shape[0]
    y = _trunk(jnp.concatenate([x_prompt, x_sample], axis=0), norm_in, w_in, a_q_norm, w_q_up, a_kv_norm, w_kv_up,
               b_q_norm, b_k_norm, c_sink, w_branch, w_out, final_norm)
    return (y[:nb], y[nb:])
```

```python
import functools

import jax
import jax.numpy as jnp
from jax import lax
from jax.experimental import pallas as pl
from jax.experimental.pallas import tpu as pltpu

D_MODEL = 1024
SEQ = 2048
DEPTH = 2
GRID_W = 64
ROPE_THETA = 10000.0
EPS = 1e-6
NEG = -1e30
HEAD_DIM = 64
LANES = 128
BRANCH_W = 256
N_BRANCH = 4
A_HEADS = 4
A_NOPE = 64
A_ROPE = 32
A_V = 64
A_Q_LORA = 256
A_KV_LORA = 128
A_IN = A_Q_LORA + A_KV_LORA + A_ROPE
QKV_W = 512
C_WINDOW = 128
D_GROUPS = ((128, 1), (512, 4), (2048, 16))
D_BAND = 64
GATE_OFF = A_IN + 2 * QKV_W + len(D_GROUPS) * QKV_W
MERGE_OFF = GATE_OFF + N_BRANCH * BRANCH_W
A_SCALE = (A_NOPE + A_ROPE) ** -0.5
QK_SCALE = HEAD_DIM ** -0.5

TM = 512
TQ = 256
VMEM_LIMIT = 56 * 1024 * 1024

F32 = jnp.float32
BF16 = jnp.bfloat16


def _dot(a, b):
    return jnp.dot(a, b, preferred_element_type=F32)


def _dot_nt(a, b):
    return lax.dot_general(a, b, (((1,), (1,)), ((), ())), preferred_element_type=F32)


def _rms(x, g):
    return x * lax.rsqrt(jnp.mean(x * x, axis=-1, keepdims=True) + EPS) * g


def _sigmoid(x):
    return 1.0 / (1.0 + jnp.exp(-x))


def _rope(a, cos, sin, shift):
    lane = lax.broadcasted_iota(jnp.int32, (1, LANES), 1)
    first = (lane % (2 * shift)) < shift
    outs = []
    for j in range(a.shape[1] // LANES):
        s = a[:, j * LANES:(j + 1) * LANES]
        partner = jnp.where(first, pltpu.roll(s, LANES - shift, 1), pltpu.roll(s, shift, 1))
        outs.append(s * cos + partner * sin)
    return outs[0] if len(outs) == 1 else jnp.concatenate(outs, axis=1)


def _head_mean_sq(x, gm):
    x2 = x * x
    hi = x2.astype(BF16)
    lo = (x2 - hi.astype(F32)).astype(BF16)
    return _dot(hi, gm) + _dot(lo, gm)


def _proj_body(x_ref, g_ref, w_ref, wq_ref, wkv_ref, gq_ref, gkv_ref, gbq_ref, gbk_ref, gm_ref, tab_ref,
               a_ref, b_ref, c_ref, d0_ref, d1_ref, d2_ref):
    xn = _rms(x_ref[...], g_ref[...]).astype(BF16)
    cos_a, sin_a = tab_ref[0], tab_ref[1]
    cos_b, sin_b = tab_ref[2], tab_ref[3]
    cos_c, sin_c = tab_ref[4], tab_ref[5]

    h = _dot(xn, w_ref[:, 0:512])
    q = _dot(_rms(h[:, 0:256], gq_ref[...]).astype(BF16), wq_ref[...])
    q = _rope(q, cos_a, sin_a, A_ROPE // 2) * A_SCALE
    kv = _dot(_rms(h[:, 256:384], gkv_ref[...]).astype(BF16), wkv_ref[...])
    kr = _rope(h[:, 384:512], cos_a, sin_a, A_ROPE // 2)
    k = kv[:, 0:512] + jnp.concatenate([kr] * A_HEADS, axis=1)
    a_ref[:, 0:512] = q.astype(BF16)
    a_ref[:, 512:1024] = k.astype(BF16)
    a_ref[:, 1024:1280] = kv[:, 512:768].astype(BF16)

    h = _dot(xn, w_ref[:, 512:1024])
    q, k = h[:, 0:256], h[:, 256:384]
    q = q * lax.rsqrt(_head_mean_sq(q, gm_ref[...]) + EPS) * gbq_ref[...]
    k = k * lax.rsqrt(_head_mean_sq(k, gm_ref[0:128, 0:128]) + EPS) * gbk_ref[...]
    b_ref[:, 0:256] = (_rope(q, cos_b, sin_b, HEAD_DIM // 4) * QK_SCALE).astype(BF16)
    b_ref[:, 256:384] = _rope(k, cos_b, sin_b, HEAD_DIM // 4).astype(BF16)
    b_ref[:, 384:512] = h[:, 384:512].astype(BF16)

    for n, o_ref in enumerate((c_ref, d0_ref, d1_ref, d2_ref)):
        h = _dot(xn, w_ref[:, 1024 + n * QKV_W:1024 + (n + 1) * QKV_W])
        o_ref[:, 0:256] = (_rope(h[:, 0:256], cos_c, sin_c, HEAD_DIM // 2) * QK_SCALE).astype(BF16)
        o_ref[:, 256:384] = _rope(h[:, 256:384], cos_c, sin_c, HEAD_DIM // 2).astype(BF16)
        o_ref[:, 384:512] = h[:, 384:512].astype(BF16)


def _const_spec(shape):
    return pl.BlockSpec(shape, lambda i: (0,) * len(shape), pipeline_mode=pl.Buffered(1))


def _project(x, lw, tables):
    n = x.shape[0]
    nt = SEQ // TM
    row = lambda w: pl.BlockSpec((TM, w), lambda i: (i, 0))
    out_w = (1280, QKV_W, QKV_W, QKV_W, QKV_W, QKV_W)
    return pl.pallas_call(
        _proj_body,
        grid=(n // TM,),
        in_specs=[
            row(D_MODEL),
            _const_spec((1, D_MODEL)),
            _const_spec((D_MODEL, 3072)),
            _const_spec((A_Q_LORA, 512)),
            _const_spec((A_KV_LORA, 768)),
            _const_spec((1, A_Q_LORA)),
            _const_spec((1, A_KV_LORA)),
            _const_spec((1, 256)),
            _const_spec((1, 128)),
            _const_spec((256, 256)),
            pl.BlockSpec((6, TM, LANES), lambda i: (0, i % nt, 0)),
        ],
        out_specs=[row(w) for w in out_w],
        out_shape=[jax.ShapeDtypeStruct((n, w), BF16) for w in out_w],
        compiler_params=pltpu.CompilerParams(dimension_semantics=("parallel",), vmem_limit_bytes=VMEM_LIMIT),
        name="proj",
    )(x, lw["g_in"], lw["w_attn"], lw["wq"], lw["wkv"], lw["gq"], lw["gkv"], lw["gbq"], lw["gbk"], lw["gm"], tables)


def _lane_lo():
    return lax.broadcasted_iota(jnp.int32, (1, LANES), 1) < HEAD_DIM


def _attend(q, k, v, bias=None, sink=None):
    s = _dot_nt(q, k)
    if bias is not None:
        s = s + bias
    m = jnp.max(s, axis=-1, keepdims=True)
    p = jnp.exp(s - m)
    l = jnp.sum(p, axis=-1, keepdims=True)
    if sink is not None:
        l = l + jnp.exp(sink - m)
    return _dot(p.astype(BF16), v), m, l


def _dense_body(q_ref, k_ref, v_ref, o_ref, *, shared):
    lo = _lane_lo()
    zero = jnp.zeros((), BF16)

    def tile(i, carry):
        r = pl.multiple_of(i * TQ, TQ)
        v = v_ref[0]
        v_a, v_b = jnp.where(lo, v, zero), jnp.where(lo, zero, v)
        if shared:
            qs = q_ref[0, pl.ds(r, TQ), :]
            q_a, q_b = jnp.where(lo, qs, zero), jnp.where(lo, zero, qs)
            k_a = k_b = k_ref[0]
        else:
            q_a, q_b = q_ref[0, pl.ds(r, TQ), 0:LANES], q_ref[0, pl.ds(r, TQ), LANES:2 * LANES]
            k_a, k_b = k_ref[0, :, 0:LANES], k_ref[0, :, LANES:2 * LANES]
        o_a, _, l_a = _attend(q_a, k_a, v_a)
        o_b, _, l_b = _attend(q_b, k_b, v_b)
        o_ref[0, pl.ds(r, TQ), :] = (o_a * (1.0 / l_a) + o_b * (1.0 / l_b)).astype(BF16)
        return carry

    lax.fori_loop(0, SEQ // TQ, tile, 0)


def _attn_params(n_grid):
    return pltpu.CompilerParams(dimension_semantics=("parallel",) * n_grid, vmem_limit_bytes=VMEM_LIMIT)


def _latent_attention(a):
    bn = a.shape[0]
    return pl.pallas_call(
        functools.partial(_dense_body, shared=False),
        grid=(bn, 2),
        in_specs=[
            pl.BlockSpec((1, SEQ, 256), lambda b, s: (b, 0, s)),
            pl.BlockSpec((1, SEQ, 256), lambda b, s: (b, 0, 2 + s)),
            pl.BlockSpec((1, SEQ, LANES), lambda b, s: (b, 0, 8 + s)),
        ],
        out_specs=pl.BlockSpec((1, SEQ, LANES), lambda b, s: (b, 0, s)),
        out_shape=jax.ShapeDtypeStruct((bn, SEQ, BRANCH_W), BF16),
        compiler_params=_attn_params(2),
        name="attn_latent",
    )(a, a, a)


def _axial_attention(qkv):
    bn = qkv.shape[0]
    return pl.pallas_call(
        functools.partial(_dense_body, shared=True),
        grid=(bn, 2),
        in_specs=[
            pl.BlockSpec((1, SEQ, LANES), lambda b, s: (b, 0, s)),
            pl.BlockSpec((1, SEQ, LANES), lambda b, s: (b, 0, 2)),
            pl.BlockSpec((1, SEQ, LANES), lambda b, s: (b, 0, 3)),
        ],
        out_specs=pl.BlockSpec((1, SEQ, LANES), lambda b, s: (b, 0, s)),
        out_shape=jax.ShapeDtypeStruct((bn, SEQ, BRANCH_W), BF16),
        compiler_params=_attn_params(2),
        name="attn_axial",
    )(qkv, qkv, qkv)


BQ = 128
BLOCKS_PER_STEP = 4


def _band_bias(q0, k0, tk, band):
    qpos = q0 + (lax.broadcasted_iota(jnp.int32, (2 * BQ, 1), 0) & (BQ - 1))
    kpos = k0 + lax.broadcasted_iota(jnp.int32, (1, tk), 1)
    return jnp.where(jnp.abs(qpos - kpos) <= band, 0.0, NEG).astype(F32)


def _banded_blocks(blocks, sinks=None):
    lo = _lane_lo()
    zero = jnp.zeros((), BF16)
    scores = []
    for q0, q1, k, _, bias in blocks:
        for first in (True, False):
            pick = (lambda x: jnp.where(lo, x, zero)) if first else (lambda x: jnp.where(lo, zero, x))
            scores.append(_dot_nt(jnp.concatenate([pick(q0), pick(q1)], axis=0), k) + bias)
    s = jnp.stack(scores)
    m = jnp.max(s, axis=-1, keepdims=True)
    p = jnp.exp(s - m)
    l = jnp.sum(p, axis=-1, keepdims=True)
    if sinks is not None:
        l = l + jnp.exp(sinks - m)
    p = p.astype(BF16)
    inv = 1.0 / l
    lse = m + jnp.log(l)
    results = []
    for b, (_, _, _, v, _) in enumerate(blocks):
        o_lo = _dot(p[2 * b], jnp.where(lo, v, zero)) * inv[2 * b]
        o_hi = _dot(p[2 * b + 1], jnp.where(lo, zero, v)) * inv[2 * b + 1]
        results.append((o_lo + o_hi, jnp.where(lo, lse[2 * b], lse[2 * b + 1])))
    return results


C_KEYS = BQ + 2 * C_WINDOW


def _window_body(sink_ref, qkv_ref, o_ref):
    row = lax.broadcasted_iota(jnp.int32, (2 * BQ, 1), 0)
    sink_kv = [jnp.where(row < BQ, sink_ref[2 * h], sink_ref[2 * h + 1]) for h in range(2)]
    sinks = jnp.stack(sink_kv * BLOCKS_PER_STEP)

    def step(i, carry):
        blocks, rows = [], []
        for j in range(BLOCKS_PER_STEP):
            r = pl.multiple_of((i * BLOCKS_PER_STEP + j) * BQ, BQ)
            ws = pl.multiple_of(jnp.clip(r - C_WINDOW, 0, SEQ - C_KEYS), BQ)
            blocks.append((qkv_ref[0, pl.ds(r, BQ), 0:128], qkv_ref[0, pl.ds(r, BQ), 128:256],
                           qkv_ref[0, pl.ds(ws, C_KEYS), 256:384], qkv_ref[0, pl.ds(ws, C_KEYS), 384:512],
                           _band_bias(r, ws, C_KEYS, C_WINDOW)))
            rows.append(r)
        for r, (out, _) in zip(rows, _banded_blocks(blocks, sinks)):
            o_ref[0, pl.ds(r, BQ), 0:128] = out[0:BQ].astype(BF16)
            o_ref[0, pl.ds(r, BQ), 128:256] = out[BQ:2 * BQ].astype(BF16)
        return carry

    lax.fori_loop(0, SEQ // (BQ * BLOCKS_PER_STEP), step, 0)


def _window_attention(qkv, sink):
    bn = qkv.shape[0]
    return pl.pallas_call(
        _window_body,
        grid=(bn,),
        in_specs=[
            pl.BlockSpec(memory_space=pltpu.MemorySpace.SMEM),
            pl.BlockSpec((1, SEQ, QKV_W), lambda b: (b, 0, 0)),
        ],
        out_specs=pl.BlockSpec((1, SEQ, BRANCH_W), lambda b: (b, 0, 0)),
        out_shape=jax.ShapeDtypeStruct((bn, SEQ, BRANCH_W), BF16),
        compiler_params=_attn_params(1),
        name="attn_window",
    )(sink, qkv)


def _dilated_body(d0_ref, d1_ref, d2_ref, o_ref, acc_ref, lse_ref):
    def block(ref, dil, c, r):
        length = SEQ // dil
        tk = min(BQ + 2 * D_BAND, length)
        if isinstance(r, int):
            ws = min(max(r - D_BAND, 0), length - tk)
        else:
            ws = pl.multiple_of(jnp.clip(r - D_BAND, 0, length - tk), D_BAND)
        base = c * QKV_W
        return (ref[0, pl.ds(r, BQ), base:base + 128], ref[0, pl.ds(r, BQ), base + 128:base + 256],
                ref[0, pl.ds(ws, tk), base + 256:base + 384], ref[0, pl.ds(ws, tk), base + 384:base + 512],
                _band_bias(r, ws, tk, D_BAND))

    def scatter(g, dil, c, r, out, lse):
        rows = pl.ds(r * dil + c, BQ, stride=dil) if dil > 1 else pl.ds(r, BQ)
        for slab in range(2):
            acc_ref[g, slab, rows, :] = out[slab * BQ:(slab + 1) * BQ]
            lse_ref[g, slab, rows, :] = lse[slab * BQ:(slab + 1) * BQ]

    def g0_step(i, carry):
        rs = [pl.multiple_of((i * BLOCKS_PER_STEP + j) * BQ, BQ) for j in range(BLOCKS_PER_STEP)]
        for r, (out, lse) in zip(rs, _banded_blocks([block(d0_ref, 1, 0, r) for r in rs])):
            scatter(0, 1, 0, r, out, lse)
        return carry

    lax.fori_loop(0, SEQ // (BQ * BLOCKS_PER_STEP), g0_step, 0)
    for c in range(4):
        rs = [j * BQ for j in range(SEQ // 4 // BQ)]
        for r, (out, lse) in zip(rs, _banded_blocks([block(d1_ref, 4, c, r) for r in rs])):
            scatter(1, 4, c, r, out, lse)
    for c0 in range(0, 16, BLOCKS_PER_STEP):
        cs = range(c0, c0 + BLOCKS_PER_STEP)
        for c, (out, lse) in zip(cs, _banded_blocks([block(d2_ref, 16, c, 0) for c in cs])):
            scatter(2, 16, c, 0, out, lse)

    def merge(i, carry):
        rows = pl.ds(pl.multiple_of(i * TQ, TQ), TQ)
        for slab in range(2):
            l0, l1, l2 = lse_ref[0, slab, rows, :], lse_ref[1, slab, rows, :], lse_ref[2, slab, rows, :]
            m = jnp.maximum(jnp.maximum(l0, l1), l2)
            w0, w1, w2 = jnp.exp(l0 - m), jnp.exp(l1 - m), jnp.exp(l2 - m)
            num = w0 * acc_ref[0, slab, rows, :] + w1 * acc_ref[1, slab, rows, :] + w2 * acc_ref[2, slab, rows, :]
            o_ref[0, rows, slab * LANES:(slab + 1) * LANES] = (num * (1.0 / (w0 + w1 + w2))).astype(BF16)
        return carry

    lax.fori_loop(0, SEQ // TQ, merge, 0)


def _dilated_attention(d0, d1, d2):
    bn = d0.shape[0]
    views = [d.reshape(bn, SEQ // dil, dil * QKV_W) for d, (_, dil) in zip((d0, d1, d2), D_GROUPS)]
    return pl.pallas_call(
        _dilated_body,
        grid=(bn,),
        in_specs=[pl.BlockSpec((1,) + v.shape[1:], lambda b: (b, 0, 0), pipeline_mode=pl.Buffered(1)) for v in views],
        out_specs=pl.BlockSpec((1, SEQ, BRANCH_W), lambda b: (b, 0, 0)),
        out_shape=jax.ShapeDtypeStruct((bn, SEQ, BRANCH_W), BF16),
        scratch_shapes=[pltpu.VMEM((3, 2, SEQ, LANES), F32), pltpu.VMEM((3, 2, SEQ, LANES), F32)],
        compiler_params=_attn_params(1),
        name="attn_dilated",
    )(*views)


def _merge_body(x_ref, g_ref, ya_ref, yb_ref, yc_ref, yd_ref, wz_ref, wmg_ref, wb_ref, wo_ref, fg_ref, o_ref, *, final):
    x = x_ref[...]
    xn = _rms(x, g_ref[...]).astype(BF16)
    merged = jnp.zeros((TM, D_MODEL), F32)
    for i, y_ref in enumerate((ya_ref, yb_ref, yc_ref, yd_ref)):
        z = _dot(xn, wz_ref[:, i * BRANCH_W:(i + 1) * BRANCH_W])
        y = y_ref[...].astype(F32) * (z * _sigmoid(z))
        branch = _dot(y.astype(BF16), wb_ref[i])
        gate = _sigmoid(_dot(xn, wmg_ref[:, i * D_MODEL:(i + 1) * D_MODEL]))
        merged = merged + gate * branch
    out = x + _dot(merged.astype(BF16), wo_ref[...])
    if final:
        out = _rms(out, fg_ref[...])
    o_ref[...] = out


def _merge(x, ys, lw, final_g, final):
    n = x.shape[0]
    row = lambda w: pl.BlockSpec((TM, w), lambda i: (i, 0))
    return pl.pallas_call(
        functools.partial(_merge_body, final=final),
        grid=(n // TM,),
        in_specs=[
            row(D_MODEL),
            _const_spec((1, D_MODEL)),
            row(BRANCH_W), row(BRANCH_W), row(BRANCH_W), row(BRANCH_W),
            _const_spec((D_MODEL, N_BRANCH * BRANCH_W)),
            _const_spec((D_MODEL, N_BRANCH * D_MODEL)),
            _const_spec((N_BRANCH, BRANCH_W, D_MODEL)),
            _const_spec((D_MODEL, D_MODEL)),
            _const_spec((1, D_MODEL)),
        ],
        out_specs=row(D_MODEL),
        out_shape=jax.ShapeDtypeStruct((n, D_MODEL), F32),
        compiler_params=pltpu.CompilerParams(dimension_semantics=("parallel",), vmem_limit_bytes=VMEM_LIMIT),
        name="merge",
    )(x, lw["g_in"], *ys, lw["wz"], lw["wmg"], lw["wb"], lw["wo"], final_g)


def _gqa_cols(w):
    q = w[..., 0:256].reshape(w.shape[:-1] + (4, HEAD_DIM))[..., (0, 2, 1, 3), :].reshape(w.shape[:-1] + (256,))
    return jnp.concatenate([q, w[..., 256:512]], axis=-1)


def _slab_heads(w, axis):
    w = jnp.moveaxis(w, axis, -1)
    w = w.reshape(w.shape[:-1] + (4, HEAD_DIM))[..., (0, 2, 1, 3), :].reshape(w.shape[:-1] + (256,))
    return jnp.moveaxis(w, -1, axis)


def _layer_weights(l, norm_in, w_in, a_q_norm, w_q_up, a_kv_norm, w_kv_up, b_q_norm, b_k_norm, w_branch, w_out):
    w = w_in[l]
    zeros = lambda c: jnp.zeros((D_MODEL, c), F32)
    off_b = A_IN
    cols = [w[:, 0:384], zeros(64), w[:, 384:416], zeros(32)]
    for n in range(2 + len(D_GROUPS)):
        cols.append(_gqa_cols(w[:, off_b + n * QKV_W:off_b + (n + 1) * QKV_W]))
    w_attn = jnp.concatenate(cols, axis=1)

    wq = w_q_up[l].reshape(A_Q_LORA, A_HEADS, A_NOPE + A_ROPE)
    wq = jnp.pad(wq, ((0, 0), (0, 0), (0, LANES - A_NOPE - A_ROPE))).reshape(A_Q_LORA, A_HEADS * LANES)
    wkv = w_kv_up[l].reshape(A_KV_LORA, A_HEADS, A_NOPE + A_V)
    wk = jnp.pad(wkv[:, :, :A_NOPE], ((0, 0), (0, 0), (0, LANES - A_NOPE))).reshape(A_KV_LORA, A_HEADS * LANES)
    wv = wkv[:, :, A_NOPE:].reshape(A_KV_LORA, A_HEADS * A_V)

    wz = w[:, GATE_OFF:MERGE_OFF]
    wz = jnp.concatenate([wz[:, 0:256]] + [_slab_heads(wz[:, i * 256:(i + 1) * 256], 1) for i in (1, 2, 3)], axis=1)
    wb = jnp.stack([w_branch[l, 0]] + [_slab_heads(w_branch[l, i], 0) for i in (1, 2, 3)])

    head_id = jnp.arange(256) // HEAD_DIM
    return {
        "g_in": norm_in[l].reshape(1, D_MODEL),
        "w_attn": w_attn.astype(BF16),
        "wq": wq.astype(BF16),
        "wkv": jnp.concatenate([wk, wv], axis=1).astype(BF16),
        "gq": a_q_norm[l].reshape(1, A_Q_LORA),
        "gkv": a_kv_norm[l].reshape(1, A_KV_LORA),
        "gbq": jnp.tile(b_q_norm[l], 4).reshape(1, 256),
        "gbk": jnp.tile(b_k_norm[l], 2).reshape(1, 128),
        "gm": jnp.where(head_id[:, None] == head_id[None, :], 1.0 / HEAD_DIM, 0.0).astype(BF16),
        "wz": wz.astype(BF16),
        "wmg": w[:, MERGE_OFF:].astype(BF16),
        "wb": wb.astype(BF16),
        "wo": w_out[l].astype(BF16),
    }


def _rope_tables():
    pos = jnp.arange(SEQ)
    rows = jnp.repeat(jnp.arange(SEQ // GRID_W), GRID_W)
    cols = jnp.tile(jnp.arange(GRID_W), SEQ // GRID_W)

    def ang(p, d):
        freqs = ROPE_THETA ** (-jnp.arange(d // 2, dtype=F32) * 2.0 / d)
        return p.astype(F32)[:, None] * freqs[None, :]

    def pair(a):
        return jnp.concatenate([jnp.cos(a), jnp.cos(a)], axis=1), jnp.concatenate([-jnp.sin(a), jnp.sin(a)], axis=1)

    one, zero = jnp.ones((SEQ, 1), F32), jnp.zeros((SEQ, 1), F32)
    ca, sa = pair(ang(pos, A_ROPE))
    cos_a = jnp.concatenate([jnp.tile(one, (1, 64)), ca, jnp.tile(one, (1, 32))], axis=1)
    sin_a = jnp.concatenate([jnp.tile(zero, (1, 64)), sa, jnp.tile(zero, (1, 32))], axis=1)
    (cr, sr), (cc, sc) = pair(ang(rows, HEAD_DIM // 2)), pair(ang(cols, HEAD_DIM // 2))
    cos_b, sin_b = jnp.tile(jnp.concatenate([cr, cc], axis=1), (1, 2)), jnp.tile(jnp.concatenate([sr, sc], axis=1), (1, 2))
    cp, sp = pair(ang(pos, HEAD_DIM))
    return jnp.stack([cos_a, sin_a, cos_b, sin_b, jnp.tile(cp, (1, 2)), jnp.tile(sp, (1, 2))])


def _trunk(x, norm_in, w_in, a_q_norm, w_q_up, a_kv_norm, w_kv_up, b_q_norm, b_k_norm, c_sink, w_branch, w_out, final_norm):
    bn = x.shape[0]
    tables = _rope_tables()
    x = x.reshape(bn * SEQ, D_MODEL)
    final_g = final_norm.reshape(1, D_MODEL)
    for l in range(DEPTH):
        lw = _layer_weights(l, norm_in, w_in, a_q_norm, w_q_up, a_kv_norm, w_kv_up, b_q_norm, b_k_norm, w_branch, w_out)
        a, b, c, d0, d1, d2 = _project(x, lw, tables)
        seq = lambda t: t.reshape(bn, SEQ, t.shape[-1])
        ys = (
            _latent_attention(seq(a)),
            _axial_attention(seq(b)),
            _window_attention(seq(c), c_sink[l]),
            _dilated_attention(seq(d0), seq(d1), seq(d2)),
        )
        ys = [y.reshape(bn * SEQ, BRANCH_W) for y in ys]
        x = _merge(x, ys, lw, final_g, final=(l == DEPTH - 1))
    return x.reshape(bn, SEQ, D_MODEL)


def kernel(x_prompt, x_sample, norm_in, w_in, a_q_norm, w_q_up, a_kv_norm, w_kv_up, b_q_norm, b_k_norm, c_sink, w_branch, w_out, final_norm):
    nb = x_prompt.shape[0]
    y = _trunk(jnp.concatenate([x_prompt, x_sample], axis=0), norm_in, w_in, a_q_norm, w_q_up, a_kv_norm, w_kv_up,
               b_q_norm, b_k_norm, c_sink, w_branch, w_out, final_norm)
    return (y[:nb], y[nb:])
```

```python
import functools

import jax
import jax.numpy as jnp
from jax import lax
from jax.experimental import pallas as pl
from jax.experimental.pallas import tpu as pltpu

D_MODEL = 1024
SEQ = 2048
DEPTH = 2
GRID_W = 64
ROPE_THETA = 10000.0
EPS = 1e-6
NEG = -1e30
HEAD_DIM = 64
LANES = 128
BRANCH_W = 256
N_BRANCH = 4
A_HEADS = 4
A_NOPE = 64
A_ROPE = 32
A_V = 64
A_Q_LORA = 256
A_KV_LORA = 128
A_IN = A_Q_LORA + A_KV_LORA + A_ROPE
QKV_W = 512
C_WINDOW = 128
D_GROUPS = ((128, 1), (512, 4), (2048, 16))
D_BAND = 64
GATE_OFF = A_IN + 2 * QKV_W + len(D_GROUPS) * QKV_W
MERGE_OFF = GATE_OFF + N_BRANCH * BRANCH_W
A_SCALE = (A_NOPE + A_ROPE) ** -0.5
QK_SCALE = HEAD_DIM ** -0.5

TM = 512
TQ = 256
VMEM_LIMIT = 56 * 1024 * 1024

F32 = jnp.float32
BF16 = jnp.bfloat16


def _dot(a, b):
    return jnp.dot(a, b, preferred_element_type=F32)


def _dot_nt(a, b):
    return lax.dot_general(a, b, (((1,), (1,)), ((), ())), preferred_element_type=F32)


def _rms(x, g):
    return x * lax.rsqrt(jnp.mean(x * x, axis=-1, keepdims=True) + EPS) * g


def _sigmoid(x):
    return 1.0 / (1.0 + jnp.exp(-x))


def _rope(a, cos, sin, shift):
    lane = lax.broadcasted_iota(jnp.int32, (1, LANES), 1)
    first = (lane % (2 * shift)) < shift
    outs = []
    for j in range(a.shape[1] // LANES):
        s = a[:, j * LANES:(j + 1) * LANES]
        partner = jnp.where(first, pltpu.roll(s, LANES - shift, 1), pltpu.roll(s, shift, 1))
        outs.append(s * cos + partner * sin)
    return outs[0] if len(outs) == 1 else jnp.concatenate(outs, axis=1)


def _head_mean_sq(x, gm):
    x2 = x * x
    hi = x2.astype(BF16)
    lo = (x2 - hi.astype(F32)).astype(BF16)
    return _dot(hi, gm) + _dot(lo, gm)


def _proj_body(x_ref, g_ref, w_ref, wq_ref, wkv_ref, gq_ref, gkv_ref, gbq_ref, gbk_ref, gm_ref, tab_ref,
               a_ref, b_ref, c_ref, d0_ref, d1_ref, d2_ref, stage_ref):
    xn = _rms(x_ref[...], g_ref[...]).astype(BF16)
    cos_a, sin_a = tab_ref[0], tab_ref[1]
    cos_b, sin_b = tab_ref[2], tab_ref[3]
    cos_c, sin_c = tab_ref[4], tab_ref[5]

    h = _dot(xn, w_ref[:, 0:512])
    q = _dot(_rms(h[:, 0:256], gq_ref[...]).astype(BF16), wq_ref[...])
    q = _rope(q, cos_a, sin_a, A_ROPE // 2) * A_SCALE
    kv = _dot(_rms(h[:, 256:384], gkv_ref[...]).astype(BF16), wkv_ref[...])
    kr = _rope(h[:, 384:512], cos_a, sin_a, A_ROPE // 2)
    k = kv[:, 0:512] + jnp.concatenate([kr] * A_HEADS, axis=1)
    a_ref[:, 0:512] = q.astype(BF16)
    a_ref[:, 512:1024] = k.astype(BF16)
    a_ref[:, 1024:1280] = kv[:, 512:768].astype(BF16)

    h = _dot(xn, w_ref[:, 512:1024])
    q, k = h[:, 0:256], h[:, 256:384]
    q = q * lax.rsqrt(_head_mean_sq(q, gm_ref[...]) + EPS) * gbq_ref[...]
    k = k * lax.rsqrt(_head_mean_sq(k, gm_ref[0:128, 0:128]) + EPS) * gbk_ref[...]
    b_ref[:, 0:256] = (_rope(q, cos_b, sin_b, HEAD_DIM // 4) * QK_SCALE).astype(BF16)
    b_ref[:, 256:384] = _rope(k, cos_b, sin_b, HEAD_DIM // 4).astype(BF16)
    b_ref[:, 384:512] = h[:, 384:512].astype(BF16)

    for n, (o_ref, dil) in enumerate(((c_ref, 1), (d0_ref, 1), (d1_ref, 4), (d2_ref, 16))):
        h = _dot(xn, w_ref[:, 1024 + n * QKV_W:1024 + (n + 1) * QKV_W])
        q = _rope(h[:, 0:256], cos_c, sin_c, HEAD_DIM // 2) * QK_SCALE
        k = _rope(h[:, 256:384], cos_c, sin_c, HEAD_DIM // 2)
        if dil == 1:
            o_ref[:, 0:256] = q.astype(BF16)
            o_ref[:, 256:384] = k.astype(BF16)
            o_ref[:, 384:512] = h[:, 384:512].astype(BF16)
        else:
            for s, slab in enumerate((q[:, 0:128], q[:, 128:256], k, h[:, 384:512])):
                stage_ref[s] = slab
            for c in range(dil):
                for s in range(QKV_W // LANES):
                    lanes = slice(c * QKV_W + s * LANES, c * QKV_W + (s + 1) * LANES)
                    o_ref[:, lanes] = stage_ref[s, pl.ds(c, TM // dil, stride=dil), :].astype(BF16)


def _const_spec(shape):
    return pl.BlockSpec(shape, lambda i: (0,) * len(shape), pipeline_mode=pl.Buffered(1))


def _project(x, lw, tables):
    n = x.shape[0]
    nt = SEQ // TM
    row = lambda w, dil=1: pl.BlockSpec((TM // dil, dil * w), lambda i: (i, 0))
    outs = ((1280, 1), (QKV_W, 1), (QKV_W, 1)) + tuple((QKV_W, dil) for _, dil in D_GROUPS)
    return pl.pallas_call(
        _proj_body,
        grid=(n // TM,),
        in_specs=[
            row(D_MODEL),
            _const_spec((1, D_MODEL)),
            _const_spec((D_MODEL, 3072)),
            _const_spec((A_Q_LORA, 512)),
            _const_spec((A_KV_LORA, 768)),
            _const_spec((1, A_Q_LORA)),
            _const_spec((1, A_KV_LORA)),
            _const_spec((1, 256)),
            _const_spec((1, 128)),
            _const_spec((256, 256)),
            pl.BlockSpec((6, TM, LANES), lambda i: (0, i % nt, 0)),
        ],
        out_specs=[row(w, dil) for w, dil in outs],
        out_shape=[jax.ShapeDtypeStruct((n // dil, dil * w), BF16) for w, dil in outs],
        scratch_shapes=[pltpu.VMEM((QKV_W // LANES, TM, LANES), F32)],
        compiler_params=pltpu.CompilerParams(dimension_semantics=("parallel",), vmem_limit_bytes=VMEM_LIMIT),
        name="proj",
    )(x, lw["g_in"], lw["w_attn"], lw["wq"], lw["wkv"], lw["gq"], lw["gkv"], lw["gbq"], lw["gbk"], lw["gm"], tables)


def _lane_lo():
    return lax.broadcasted_iota(jnp.int32, (1, LANES), 1) < HEAD_DIM


def _attend(q, k, v, bias=None, sink=None):
    s = _dot_nt(q, k)
    if bias is not None:
        s = s + bias
    m = jnp.max(s, axis=-1, keepdims=True)
    p = jnp.exp(s - m)
    l = jnp.sum(p, axis=-1, keepdims=True)
    if sink is not None:
        l = l + jnp.exp(sink - m)
    return _dot(p.astype(BF16), v), m, l


def _dense_body(q_ref, k_ref, v_ref, o_ref, *, shared):
    lo = _lane_lo()
    zero = jnp.zeros((), BF16)

    def tile(i, carry):
        r = pl.multiple_of(i * TQ, TQ)
        v = v_ref[0]
        v_a, v_b = jnp.where(lo, v, zero), jnp.where(lo, zero, v)
        if shared:
            qs = q_ref[0, pl.ds(r, TQ), :]
            q_a, q_b = jnp.where(lo, qs, zero), jnp.where(lo, zero, qs)
            k_a = k_b = k_ref[0]
        else:
            q_a, q_b = q_ref[0, pl.ds(r, TQ), 0:LANES], q_ref[0, pl.ds(r, TQ), LANES:2 * LANES]
            k_a, k_b = k_ref[0, :, 0:LANES], k_ref[0, :, LANES:2 * LANES]
        o_a, _, l_a = _attend(q_a, k_a, v_a)
        o_b, _, l_b = _attend(q_b, k_b, v_b)
        o_ref[0, pl.ds(r, TQ), :] = (o_a * (1.0 / l_a) + o_b * (1.0 / l_b)).astype(BF16)
        return carry

    lax.fori_loop(0, SEQ // TQ, tile, 0)


def _attn_params(n_grid):
    return pltpu.CompilerParams(dimension_semantics=("parallel",) * n_grid, vmem_limit_bytes=VMEM_LIMIT)


def _latent_attention(a):
    bn = a.shape[0]
    return pl.pallas_call(
        functools.partial(_dense_body, shared=False),
        grid=(bn, 2),
        in_specs=[
            pl.BlockSpec((1, SEQ, 256), lambda b, s: (b, 0, s)),
            pl.BlockSpec((1, SEQ, 256), lambda b, s: (b, 0, 2 + s)),
            pl.BlockSpec((1, SEQ, LANES), lambda b, s: (b, 0, 8 + s)),
        ],
        out_specs=pl.BlockSpec((1, SEQ, LANES), lambda b, s: (b, 0, s)),
        out_shape=jax.ShapeDtypeStruct((bn, SEQ, BRANCH_W), BF16),
        compiler_params=_attn_params(2),
        name="attn_latent",
    )(a, a, a)


def _axial_attention(qkv):
    bn = qkv.shape[0]
    return pl.pallas_call(
        functools.partial(_dense_body, shared=True),
        grid=(bn, 2),
        in_specs=[
            pl.BlockSpec((1, SEQ, LANES), lambda b, s: (b, 0, s)),
            pl.BlockSpec((1, SEQ, LANES), lambda b, s: (b, 0, 2)),
            pl.BlockSpec((1, SEQ, LANES), lambda b, s: (b, 0, 3)),
        ],
        out_specs=pl.BlockSpec((1, SEQ, LANES), lambda b, s: (b, 0, s)),
        out_shape=jax.ShapeDtypeStruct((bn, SEQ, BRANCH_W), BF16),
        compiler_params=_attn_params(2),
        name="attn_axial",
    )(qkv, qkv, qkv)


BQ = 128
BLOCKS_PER_STEP = 4


def _band_bias(q0, k0, tk, band):
    qpos = q0 + (lax.broadcasted_iota(jnp.int32, (2 * BQ, 1), 0) & (BQ - 1))
    kpos = k0 + lax.broadcasted_iota(jnp.int32, (1, tk), 1)
    return jnp.where(jnp.abs(qpos - kpos) <= band, 0.0, NEG).astype(F32)


def _banded_blocks(blocks, sinks=None):
    lo = _lane_lo()
    zero = jnp.zeros((), BF16)
    scores = []
    for q0, q1, k, _, bias in blocks:
        for first in (True, False):
            pick = (lambda x: jnp.where(lo, x, zero)) if first else (lambda x: jnp.where(lo, zero, x))
            scores.append(_dot_nt(jnp.concatenate([pick(q0), pick(q1)], axis=0), k) + bias)
    s = jnp.stack(scores)
    m = jnp.max(s, axis=-1, keepdims=True)
    p = jnp.exp(s - m)
    l = jnp.sum(p, axis=-1, keepdims=True)
    if sinks is not None:
        l = l + jnp.exp(sinks - m)
    p = p.astype(BF16)
    inv = 1.0 / l
    lse = m + jnp.log(l)
    results = []
    for b, (_, _, _, v, _) in enumerate(blocks):
        o_lo = _dot(p[2 * b], jnp.where(lo, v, zero)) * inv[2 * b]
        o_hi = _dot(p[2 * b + 1], jnp.where(lo, zero, v)) * inv[2 * b + 1]
        results.append((o_lo + o_hi, jnp.where(lo, lse[2 * b], lse[2 * b + 1])))
    return results


C_KEYS = BQ + 2 * C_WINDOW


def _window_body(sink_ref, qkv_ref, o_ref):
    row = lax.broadcasted_iota(jnp.int32, (2 * BQ, 1), 0)
    sink_kv = [jnp.where(row < BQ, sink_ref[2 * h], sink_ref[2 * h + 1]) for h in range(2)]
    sinks = jnp.stack(sink_kv * BLOCKS_PER_STEP)

    def step(i, carry):
        blocks, rows = [], []
        for j in range(BLOCKS_PER_STEP):
            r = pl.multiple_of((i * BLOCKS_PER_STEP + j) * BQ, BQ)
            ws = pl.multiple_of(jnp.clip(r - C_WINDOW, 0, SEQ - C_KEYS), BQ)
            blocks.append((qkv_ref[0, pl.ds(r, BQ), 0:128], qkv_ref[0, pl.ds(r, BQ), 128:256],
                           qkv_ref[0, pl.ds(ws, C_KEYS), 256:384], qkv_ref[0, pl.ds(ws, C_KEYS), 384:512],
                           _band_bias(r, ws, C_KEYS, C_WINDOW)))
            rows.append(r)
        for r, (out, _) in zip(rows, _banded_blocks(blocks, sinks)):
            o_ref[0, pl.ds(r, BQ), 0:128] = out[0:BQ].astype(BF16)
            o_ref[0, pl.ds(r, BQ), 128:256] = out[BQ:2 * BQ].astype(BF16)
        return carry

    lax.fori_loop(0, SEQ // (BQ * BLOCKS_PER_STEP), step, 0)


def _window_attention(qkv, sink):
    bn = qkv.shape[0]
    return pl.pallas_call(
        _window_body,
        grid=(bn,),
        in_specs=[
            pl.BlockSpec(memory_space=pltpu.MemorySpace.SMEM),
            pl.BlockSpec((1, SEQ, QKV_W), lambda b: (b, 0, 0)),
        ],
        out_specs=pl.BlockSpec((1, SEQ, BRANCH_W), lambda b: (b, 0, 0)),
        out_shape=jax.ShapeDtypeStruct((bn, SEQ, BRANCH_W), BF16),
        compiler_params=_attn_params(1),
        name="attn_window",
    )(sink, qkv)


def _dilated_body(d0_ref, d1_ref, d2_ref, o_ref, acc_ref, lse_ref):
    def block(ref, dil, c, r):
        length = SEQ // dil
        tk = min(BQ + 2 * D_BAND, length)
        if isinstance(r, int):
            ws = min(max(r - D_BAND, 0), length - tk)
        else:
            ws = pl.multiple_of(jnp.clip(r - D_BAND, 0, length - tk), D_BAND)
        base = c * QKV_W
        return (ref[0, pl.ds(r, BQ), base:base + 128], ref[0, pl.ds(r, BQ), base + 128:base + 256],
                ref[0, pl.ds(ws, tk), base + 256:base + 384], ref[0, pl.ds(ws, tk), base + 384:base + 512],
                _band_bias(r, ws, tk, D_BAND))

    def scatter(g, dil, c, r, out, lse):
        rows = pl.ds(r * dil + c, BQ, stride=dil) if dil > 1 else pl.ds(r, BQ)
        for slab in range(2):
            acc_ref[g, slab, rows, :] = out[slab * BQ:(slab + 1) * BQ]
            lse_ref[g, slab, rows, :] = lse[slab * BQ:(slab + 1) * BQ]

    def g0_step(i, carry):
        rs = [pl.multiple_of((i * BLOCKS_PER_STEP + j) * BQ, BQ) for j in range(BLOCKS_PER_STEP)]
        for r, (out, lse) in zip(rs, _banded_blocks([block(d0_ref, 1, 0, r) for r in rs])):
            scatter(0, 1, 0, r, out, lse)
        return carry

    lax.fori_loop(0, SEQ // (BQ * BLOCKS_PER_STEP), g0_step, 0)
    for c in range(4):
        rs = [j * BQ for j in range(SEQ // 4 // BQ)]
        for r, (out, lse) in zip(rs, _banded_blocks([block(d1_ref, 4, c, r) for r in rs])):
            scatter(1, 4, c, r, out, lse)
    for c0 in range(0, 16, BLOCKS_PER_STEP):
        cs = range(c0, c0 + BLOCKS_PER_STEP)
        for c, (out, lse) in zip(cs, _banded_blocks([block(d2_ref, 16, c, 0) for c in cs])):
            scatter(2, 16, c, 0, out, lse)

    def merge(i, carry):
        rows = pl.ds(pl.multiple_of(i * TQ, TQ), TQ)
        for slab in range(2):
            l0, l1, l2 = lse_ref[0, slab, rows, :], lse_ref[1, slab, rows, :], lse_ref[2, slab, rows, :]
            m = jnp.maximum(jnp.maximum(l0, l1), l2)
            w0, w1, w2 = jnp.exp(l0 - m), jnp.exp(l1 - m), jnp.exp(l2 - m)
            num = w0 * acc_ref[0, slab, rows, :] + w1 * acc_ref[1, slab, rows, :] + w2 * acc_ref[2, slab, rows, :]
            o_ref[0, rows, slab * LANES:(slab + 1) * LANES] = (num * (1.0 / (w0 + w1 + w2))).astype(BF16)
        return carry

    lax.fori_loop(0, SEQ // TQ, merge, 0)


def _dilated_attention(views):
    bn = views[0].shape[0]
    return pl.pallas_call(
        _dilated_body,
        grid=(bn,),
        in_specs=[pl.BlockSpec((1,) + v.shape[1:], lambda b: (b, 0, 0), pipeline_mode=pl.Buffered(1)) for v in views],
        out_specs=pl.BlockSpec((1, SEQ, BRANCH_W), lambda b: (b, 0, 0)),
        out_shape=jax.ShapeDtypeStruct((bn, SEQ, BRANCH_W), BF16),
        scratch_shapes=[pltpu.VMEM((3, 2, SEQ, LANES), F32), pltpu.VMEM((3, 2, SEQ, LANES), F32)],
        compiler_params=_attn_params(1),
        name="attn_dilated",
    )(*views)


def _merge_body(x_ref, g_ref, ya_ref, yb_ref, yc_ref, yd_ref, wz_ref, wmg_ref, wb_ref, wo_ref, fg_ref, o_ref, *, final):
    x = x_ref[...]
    xn = _rms(x, g_ref[...]).astype(BF16)
    merged = jnp.zeros((TM, D_MODEL), F32)
    for i, y_ref in enumerate((ya_ref, yb_ref, yc_ref, yd_ref)):
        z = _dot(xn, wz_ref[:, i * BRANCH_W:(i + 1) * BRANCH_W])
        y = y_ref[...].astype(F32) * (z * _sigmoid(z))
        branch = _dot(y.astype(BF16), wb_ref[i])
        gate = _sigmoid(_dot(xn, wmg_ref[:, i * D_MODEL:(i + 1) * D_MODEL]))
        merged = merged + gate * branch
    out = x + _dot(merged.astype(BF16), wo_ref[...])
    if final:
        out = _rms(out, fg_ref[...])
    o_ref[...] = out


def _merge(x, ys, lw, final_g, final):
    n = x.shape[0]
    row = lambda w: pl.BlockSpec((TM, w), lambda i: (i, 0))
    return pl.pallas_call(
        functools.partial(_merge_body, final=final),
        grid=(n // TM,),
        in_specs=[
            row(D_MODEL),
            _const_spec((1, D_MODEL)),
            row(BRANCH_W), row(BRANCH_W), row(BRANCH_W), row(BRANCH_W),
            _const_spec((D_MODEL, N_BRANCH * BRANCH_W)),
            _const_spec((D_MODEL, N_BRANCH * D_MODEL)),
            _const_spec((N_BRANCH, BRANCH_W, D_MODEL)),
            _const_spec((D_MODEL, D_MODEL)),
            _const_spec((1, D_MODEL)),
        ],
        out_specs=row(D_MODEL),
        out_shape=jax.ShapeDtypeStruct((n, D_MODEL), F32),
        compiler_params=pltpu.CompilerParams(dimension_semantics=("parallel",), vmem_limit_bytes=VMEM_LIMIT),
        name="merge",
    )(x, lw["g_in"], *ys, lw["wz"], lw["wmg"], lw["wb"], lw["wo"], final_g)


def _gqa_cols(w):
    q = w[..., 0:256].reshape(w.shape[:-1] + (4, HEAD_DIM))[..., (0, 2, 1, 3), :].reshape(w.shape[:-1] + (256,))
    return jnp.concatenate([q, w[..., 256:512]], axis=-1)


def _slab_heads(w, axis):
    w = jnp.moveaxis(w, axis, -1)
    w = w.reshape(w.shape[:-1] + (4, HEAD_DIM))[..., (0, 2, 1, 3), :].reshape(w.shape[:-1] + (256,))
    return jnp.moveaxis(w, -1, axis)


def _layer_weights(l, norm_in, w_in, a_q_norm, w_q_up, a_kv_norm, w_kv_up, b_q_norm, b_k_norm, w_branch, w_out):
    w = w_in[l]
    zeros = lambda c: jnp.zeros((D_MODEL, c), F32)
    off_b = A_IN
    cols = [w[:, 0:384], zeros(64), w[:, 384:416], zeros(32)]
    for n in range(2 + len(D_GROUPS)):
        cols.append(_gqa_cols(w[:, off_b + n * QKV_W:off_b + (n + 1) * QKV_W]))
    w_attn = jnp.concatenate(cols, axis=1)

    wq = w_q_up[l].reshape(A_Q_LORA, A_HEADS, A_NOPE + A_ROPE)
    wq = jnp.pad(wq, ((0, 0), (0, 0), (0, LANES - A_NOPE - A_ROPE))).reshape(A_Q_LORA, A_HEADS * LANES)
    wkv = w_kv_up[l].reshape(A_KV_LORA, A_HEADS, A_NOPE + A_V)
    wk = jnp.pad(wkv[:, :, :A_NOPE], ((0, 0), (0, 0), (0, LANES - A_NOPE))).reshape(A_KV_LORA, A_HEADS * LANES)
    wv = wkv[:, :, A_NOPE:].reshape(A_KV_LORA, A_HEADS * A_V)

    wz = w[:, GATE_OFF:MERGE_OFF]
    wz = jnp.concatenate([wz[:, 0:256]] + [_slab_heads(wz[:, i * 256:(i + 1) * 256], 1) for i in (1, 2, 3)], axis=1)
    wb = jnp.stack([w_branch[l, 0]] + [_slab_heads(w_branch[l, i], 0) for i in (1, 2, 3)])

    head_id = jnp.arange(256) // HEAD_DIM
    return {
        "g_in": norm_in[l].reshape(1, D_MODEL),
        "w_attn": w_attn.astype(BF16),
        "wq": wq.astype(BF16),
        "wkv": jnp.concatenate([wk, wv], axis=1).astype(BF16),
        "gq": a_q_norm[l].reshape(1, A_Q_LORA),
        "gkv": a_kv_norm[l].reshape(1, A_KV_LORA),
        "gbq": jnp.tile(b_q_norm[l], 4).reshape(1, 256),
        "gbk": jnp.tile(b_k_norm[l], 2).reshape(1, 128),
        "gm": jnp.where(head_id[:, None] == head_id[None, :], 1.0 / HEAD_DIM, 0.0).astype(BF16),
        "wz": wz.astype(BF16),
        "wmg": w[:, MERGE_OFF:].astype(BF16),
        "wb": wb.astype(BF16),
        "wo": w_out[l].astype(BF16),
    }


def _rope_tables():
    pos = jnp.arange(SEQ)
    rows = jnp.repeat(jnp.arange(SEQ // GRID_W), GRID_W)
    cols = jnp.tile(jnp.arange(GRID_W), SEQ // GRID_W)

    def ang(p, d):
        freqs = ROPE_THETA ** (-jnp.arange(d // 2, dtype=F32) * 2.0 / d)
        return p.astype(F32)[:, None] * freqs[None, :]

    def pair(a):
        return jnp.concatenate([jnp.cos(a), jnp.cos(a)], axis=1), jnp.concatenate([-jnp.sin(a), jnp.sin(a)], axis=1)

    one, zero = jnp.ones((SEQ, 1), F32), jnp.zeros((SEQ, 1), F32)
    ca, sa = pair(ang(pos, A_ROPE))
    cos_a = jnp.concatenate([jnp.tile(one, (1, 64)), ca, jnp.tile(one, (1, 32))], axis=1)
    sin_a = jnp.concatenate([jnp.tile(zero, (1, 64)), sa, jnp.tile(zero, (1, 32))], axis=1)
    (cr, sr), (cc, sc) = pair(ang(rows, HEAD_DIM // 2)), pair(ang(cols, HEAD_DIM // 2))
    cos_b, sin_b = jnp.tile(jnp.concatenate([cr, cc], axis=1), (1, 2)), jnp.tile(jnp.concatenate([sr, sc], axis=1), (1, 2))
    cp, sp = pair(ang(pos, HEAD_DIM))
    return jnp.stack([cos_a, sin_a, cos_b, sin_b, jnp.tile(cp, (1, 2)), jnp.tile(sp, (1, 2))])


def _trunk_prepared(x, layers, tables, c_sink, final_g):
    bn = x.shape[0]
    x = x.reshape(bn * SEQ, D_MODEL)
    for l, lw in enumerate(layers):
        a, b, c, d0, d1, d2 = _project(x, lw, tables)
        seq = lambda t: t.reshape(bn, -1, t.shape[-1])
        ys = (
            _latent_attention(seq(a)),
            _axial_attention(seq(b)),
            _window_attention(seq(c), c_sink[l]),
            _dilated_attention([seq(d0), seq(d1), seq(d2)]),
        )
        ys = [y.reshape(bn * SEQ, BRANCH_W) for y in ys]
        x = _merge(x, ys, lw, final_g, final=(l == DEPTH - 1))
    return x.reshape(bn, SEQ, D_MODEL)


def _prepare(norm_in, w_in, a_q_norm, w_q_up, a_kv_norm, w_kv_up, b_q_norm, b_k_norm, c_sink, w_branch, w_out, final_norm):
    layers = [_layer_weights(l, norm_in, w_in, a_q_norm, w_q_up, a_kv_norm, w_kv_up, b_q_norm, b_k_norm, w_branch, w_out)
              for l in range(DEPTH)]
    return layers, _rope_tables(), c_sink, final_norm.reshape(1, D_MODEL)


def _trunk(x, *params):
    return _trunk_prepared(x, *_prepare(*params))


def kernel(x_prompt, x_sample, norm_in, w_in, a_q_norm, w_q_up, a_kv_norm, w_kv_up, b_q_norm, b_k_norm, c_sink, w_branch, w_out, final_norm):
    prepared = _prepare(norm_in, w_in, a_q_norm, w_q_up, a_kv_norm, w_kv_up, b_q_norm, b_k_norm, c_sink, w_branch, w_out, final_norm)
    return (_trunk_prepared(x_prompt, *prepared), _trunk_prepared(x_sample, *prepared))
```

```python
import functools

import jax
import jax.numpy as jnp
from jax import lax
from jax.experimental import pallas as pl
from jax.experimental.pallas import tpu as pltpu

D_MODEL = 1024
SEQ = 2048
DEPTH = 2
GRID_W = 64
ROPE_THETA = 10000.0
EPS = 1e-6
NEG = -1e30
HEAD_DIM = 64
LANES = 128
BRANCH_W = 256
N_BRANCH = 4
A_HEADS = 4
A_NOPE = 64
A_ROPE = 32
A_V = 64
A_Q_LORA = 256
A_KV_LORA = 128
A_IN = A_Q_LORA + A_KV_LORA + A_ROPE
QKV_W = 512
C_WINDOW = 128
D_GROUPS = ((128, 1), (512, 4), (2048, 16))
D_BAND = 64
GATE_OFF = A_IN + 2 * QKV_W + len(D_GROUPS) * QKV_W
MERGE_OFF = GATE_OFF + N_BRANCH * BRANCH_W
LOG2E = 1.4426950408889634
A_SCALE = (A_NOPE + A_ROPE) ** -0.5 * LOG2E
QK_SCALE = HEAD_DIM ** -0.5 * LOG2E

TM = 1024
TQ = 256
VMEM_LIMIT = 56 * 1024 * 1024

F32 = jnp.float32
BF16 = jnp.bfloat16


def _dot(a, b):
    return jnp.dot(a, b, preferred_element_type=F32)


def _dot_nt(a, b):
    return lax.dot_general(a, b, (((1,), (1,)), ((), ())), preferred_element_type=F32)


def _rms(x, g):
    return x * lax.rsqrt(jnp.mean(x * x, axis=-1, keepdims=True) + EPS) * g


def _sigmoid(x):
    return 1.0 / (1.0 + jnp.exp(-x))


def _rope(a, cos, sin, shift):
    lane = lax.broadcasted_iota(jnp.int32, (1, LANES), 1)
    first = (lane % (2 * shift)) < shift
    outs = []
    for j in range(a.shape[1] // LANES):
        s = a[:, j * LANES:(j + 1) * LANES]
        partner = jnp.where(first, pltpu.roll(s, LANES - shift, 1), pltpu.roll(s, shift, 1))
        outs.append(s * cos + partner * sin)
    return outs[0] if len(outs) == 1 else jnp.concatenate(outs, axis=1)


def _head_mean_sq(x, gm):
    x2 = x * x
    hi = x2.astype(BF16)
    lo = (x2 - hi.astype(F32)).astype(BF16)
    return _dot(hi, gm) + _dot(lo, gm)


def _proj_body(x_ref, g_ref, w_ref, wq_ref, wkv_ref, gq_ref, gkv_ref, gbq_ref, gbk_ref, gm_ref, tab_ref,
               a_ref, b_ref, c_ref, d0_ref, d1_ref, d2_ref, stage_ref):
    xn = _rms(x_ref[...], g_ref[...]).astype(BF16)
    cos_a, sin_a = tab_ref[0], tab_ref[1]
    cos_b, sin_b = tab_ref[2], tab_ref[3]
    cos_c, sin_c = tab_ref[4], tab_ref[5]

    h = _dot(xn, w_ref[:, 0:512])
    q = _dot(_rms(h[:, 0:256], gq_ref[...]).astype(BF16), wq_ref[...])
    q = _rope(q, cos_a, sin_a, A_ROPE // 2) * A_SCALE
    kv = _dot(_rms(h[:, 256:384], gkv_ref[...]).astype(BF16), wkv_ref[...])
    kr = _rope(h[:, 384:512], cos_a, sin_a, A_ROPE // 2)
    k = kv[:, 0:512] + jnp.concatenate([kr] * A_HEADS, axis=1)
    a_ref[:, 0:512] = q.astype(BF16)
    a_ref[:, 512:1024] = k.astype(BF16)
    a_ref[:, 1024:1280] = kv[:, 512:768].astype(BF16)

    h = _dot(xn, w_ref[:, 512:1024])
    q, k = h[:, 0:256], h[:, 256:384]
    q = q * lax.rsqrt(_head_mean_sq(q, gm_ref[...]) + EPS) * gbq_ref[...]
    k = k * lax.rsqrt(_head_mean_sq(k, gm_ref[0:128, 0:128]) + EPS) * gbk_ref[...]
    b_ref[:, 0:256] = (_rope(q, cos_b, sin_b, HEAD_DIM // 4) * QK_SCALE).astype(BF16)
    b_ref[:, 256:384] = _rope(k, cos_b, sin_b, HEAD_DIM // 4).astype(BF16)
    b_ref[:, 384:512] = h[:, 384:512].astype(BF16)

    for n, (o_ref, dil) in enumerate(((c_ref, 1), (d0_ref, 1), (d1_ref, 4), (d2_ref, 16))):
        h = _dot(xn, w_ref[:, 1024 + n * QKV_W:1024 + (n + 1) * QKV_W])
        q = _rope(h[:, 0:256], cos_c, sin_c, HEAD_DIM // 2) * QK_SCALE
        k = _rope(h[:, 256:384], cos_c, sin_c, HEAD_DIM // 2)
        if dil == 1:
            o_ref[:, 0:256] = q.astype(BF16)
            o_ref[:, 256:384] = k.astype(BF16)
            o_ref[:, 384:512] = h[:, 384:512].astype(BF16)
        else:
            for s, slab in enumerate((q[:, 0:128], q[:, 128:256], k, h[:, 384:512])):
                stage_ref[s] = slab
            for c in range(dil):
                for s in range(QKV_W // LANES):
                    lanes = slice(c * QKV_W + s * LANES, c * QKV_W + (s + 1) * LANES)
                    o_ref[:, lanes] = stage_ref[s, pl.ds(c, TM // dil, stride=dil), :].astype(BF16)


def _const_spec(shape):
    return pl.BlockSpec(shape, lambda i: (0,) * len(shape), pipeline_mode=pl.Buffered(1))


def _project(x, lw, tables):
    n = x.shape[0]
    nt = SEQ // TM
    row = lambda w, dil=1: pl.BlockSpec((TM // dil, dil * w), lambda i: (i, 0))
    outs = ((1280, 1), (QKV_W, 1), (QKV_W, 1)) + tuple((QKV_W, dil) for _, dil in D_GROUPS)
    return pl.pallas_call(
        _proj_body,
        grid=(n // TM,),
        in_specs=[
            row(D_MODEL),
            _const_spec((1, D_MODEL)),
            _const_spec((D_MODEL, 3072)),
            _const_spec((A_Q_LORA, 512)),
            _const_spec((A_KV_LORA, 768)),
            _const_spec((1, A_Q_LORA)),
            _const_spec((1, A_KV_LORA)),
            _const_spec((1, 256)),
            _const_spec((1, 128)),
            _const_spec((256, 256)),
            pl.BlockSpec((6, TM, LANES), lambda i: (0, i % nt, 0)),
        ],
        out_specs=[row(w, dil) for w, dil in outs],
        out_shape=[jax.ShapeDtypeStruct((n // dil, dil * w), BF16) for w, dil in outs],
        scratch_shapes=[pltpu.VMEM((QKV_W // LANES, TM, LANES), F32)],
        compiler_params=pltpu.CompilerParams(dimension_semantics=("parallel",), vmem_limit_bytes=VMEM_LIMIT),
        name="proj",
    )(x, lw["g_in"], lw["w_attn"], lw["wq"], lw["wkv"], lw["gq"], lw["gkv"], lw["gbq"], lw["gbk"], lw["gm"], tables)


def _lane_lo():
    return lax.broadcasted_iota(jnp.int32, (1, LANES), 1) < HEAD_DIM


def _attend(q, k, v):
    s = _dot_nt(q, k)
    p = jnp.exp2(s - jnp.max(s, axis=-1, keepdims=True))
    return _dot(p.astype(BF16), v), jnp.sum(p, axis=-1, keepdims=True)


def _dense_body(q_ref, k_ref, v_ref, o_ref, vm_ref, *, shared):
    lo = _lane_lo()
    zero = jnp.zeros((), BF16)
    v = v_ref[0]
    vm_ref[0] = jnp.where(lo, v, zero)
    vm_ref[1] = jnp.where(lo, zero, v)

    def tile(i, carry):
        r = pl.multiple_of(i * TQ, TQ)
        if shared:
            qs = q_ref[0, pl.ds(r, TQ), :]
            q_a, q_b = jnp.where(lo, qs, zero), jnp.where(lo, zero, qs)
            k_a = k_b = k_ref[0]
        else:
            q_a, q_b = q_ref[0, pl.ds(r, TQ), 0:LANES], q_ref[0, pl.ds(r, TQ), LANES:2 * LANES]
            k_a, k_b = k_ref[0, :, 0:LANES], k_ref[0, :, LANES:2 * LANES]
        o_a, l_a = _attend(q_a, k_a, vm_ref[0])
        o_b, l_b = _attend(q_b, k_b, vm_ref[1])
        o_ref[0, pl.ds(r, TQ), :] = (o_a * (1.0 / l_a) + o_b * (1.0 / l_b)).astype(BF16)
        return carry

    lax.fori_loop(0, SEQ // TQ, tile, 0, unroll=2)


def _attn_params(n_grid):
    return pltpu.CompilerParams(dimension_semantics=("parallel",) * n_grid, vmem_limit_bytes=VMEM_LIMIT)


def _latent_attention(a):
    bn = a.shape[0]
    return pl.pallas_call(
        functools.partial(_dense_body, shared=False),
        grid=(bn, 2),
        in_specs=[
            pl.BlockSpec((1, SEQ, 256), lambda b, s: (b, 0, s)),
            pl.BlockSpec((1, SEQ, 256), lambda b, s: (b, 0, 2 + s)),
            pl.BlockSpec((1, SEQ, LANES), lambda b, s: (b, 0, 8 + s)),
        ],
        out_specs=pl.BlockSpec((1, SEQ, LANES), lambda b, s: (b, 0, s)),
        out_shape=jax.ShapeDtypeStruct((bn, SEQ, BRANCH_W), BF16),
        scratch_shapes=[pltpu.VMEM((2, SEQ, LANES), BF16)],
        compiler_params=_attn_params(2),
        name="attn_latent",
    )(a, a, a)


def _axial_attention(qkv):
    bn = qkv.shape[0]
    return pl.pallas_call(
        functools.partial(_dense_body, shared=True),
        grid=(bn, 2),
        in_specs=[
            pl.BlockSpec((1, SEQ, LANES), lambda b, s: (b, 0, s)),
            pl.BlockSpec((1, SEQ, LANES), lambda b, s: (b, 0, 2)),
            pl.BlockSpec((1, SEQ, LANES), lambda b, s: (b, 0, 3)),
        ],
        out_specs=pl.BlockSpec((1, SEQ, LANES), lambda b, s: (b, 0, s)),
        out_shape=jax.ShapeDtypeStruct((bn, SEQ, BRANCH_W), BF16),
        scratch_shapes=[pltpu.VMEM((2, SEQ, LANES), BF16)],
        compiler_params=_attn_params(2),
        name="attn_axial",
    )(qkv, qkv, qkv)


BQ = 128
BLOCKS_PER_STEP = 4


def _band_bias(q0, k0, tk, band):
    qpos = q0 + (lax.broadcasted_iota(jnp.int32, (2 * BQ, 1), 0) & (BQ - 1))
    kpos = k0 + lax.broadcasted_iota(jnp.int32, (1, tk), 1)
    return jnp.where(jnp.abs(qpos - kpos) <= band, 0.0, NEG).astype(F32)


def _banded_blocks(blocks, sinks=None):
    lo = _lane_lo()
    zero = jnp.zeros((), BF16)
    scores = []
    for q0, q1, k, _, bias in blocks:
        for first in (True, False):
            pick = (lambda x: jnp.where(lo, x, zero)) if first else (lambda x: jnp.where(lo, zero, x))
            scores.append(_dot_nt(jnp.concatenate([pick(q0), pick(q1)], axis=0), k) + bias)
    s = jnp.stack(scores)
    m = jnp.max(s, axis=-1, keepdims=True)
    p = jnp.exp2(s - m)
    l = jnp.sum(p, axis=-1, keepdims=True)
    if sinks is not None:
        l = l + jnp.exp2(sinks - m)
    p = p.astype(BF16)
    inv = 1.0 / l
    lse = m + jnp.log2(l)
    results = []
    for b, (_, _, _, v, _) in enumerate(blocks):
        o_lo = _dot(p[2 * b], jnp.where(lo, v, zero)) * inv[2 * b]
        o_hi = _dot(p[2 * b + 1], jnp.where(lo, zero, v)) * inv[2 * b + 1]
        results.append((o_lo + o_hi, jnp.where(lo, lse[2 * b], lse[2 * b + 1])))
    return results


C_KEYS = BQ + 2 * C_WINDOW


def _window_body(sink_ref, qkv_ref, o_ref):
    row = lax.broadcasted_iota(jnp.int32, (2 * BQ, 1), 0)
    sink_kv = [jnp.where(row < BQ, sink_ref[2 * h], sink_ref[2 * h + 1]) * LOG2E for h in range(2)]
    sinks = jnp.stack(sink_kv * BLOCKS_PER_STEP)

    def step(i, carry):
        blocks, rows = [], []
        for j in range(BLOCKS_PER_STEP):
            r = pl.multiple_of((i * BLOCKS_PER_STEP + j) * BQ, BQ)
            ws = pl.multiple_of(jnp.clip(r - C_WINDOW, 0, SEQ - C_KEYS), BQ)
            blocks.append((qkv_ref[0, pl.ds(r, BQ), 0:128], qkv_ref[0, pl.ds(r, BQ), 128:256],
                           qkv_ref[0, pl.ds(ws, C_KEYS), 256:384], qkv_ref[0, pl.ds(ws, C_KEYS), 384:512],
                           _band_bias(r, ws, C_KEYS, C_WINDOW)))
            rows.append(r)
        for r, (out, _) in zip(rows, _banded_blocks(blocks, sinks)):
            o_ref[0, pl.ds(r, BQ), 0:128] = out[0:BQ].astype(BF16)
            o_ref[0, pl.ds(r, BQ), 128:256] = out[BQ:2 * BQ].astype(BF16)
        return carry

    lax.fori_loop(0, SEQ // (BQ * BLOCKS_PER_STEP), step, 0)


def _window_attention(qkv, sink):
    bn = qkv.shape[0]
    return pl.pallas_call(
        _window_body,
        grid=(bn,),
        in_specs=[
            pl.BlockSpec(memory_space=pltpu.MemorySpace.SMEM),
            pl.BlockSpec((1, SEQ, QKV_W), lambda b: (b, 0, 0)),
        ],
        out_specs=pl.BlockSpec((1, SEQ, BRANCH_W), lambda b: (b, 0, 0)),
        out_shape=jax.ShapeDtypeStruct((bn, SEQ, BRANCH_W), BF16),
        compiler_params=_attn_params(1),
        name="attn_window",
    )(sink, qkv)


def _dilated_body(d0_ref, d1_ref, d2_ref, o_ref, acc_ref, lse_ref):
    def block(ref, dil, c, r):
        length = SEQ // dil
        tk = min(BQ + 2 * D_BAND, length)
        if isinstance(r, int):
            ws = min(max(r - D_BAND, 0), length - tk)
        else:
            ws = pl.multiple_of(jnp.clip(r - D_BAND, 0, length - tk), D_BAND)
        base = c * QKV_W
        return (ref[0, pl.ds(r, BQ), base:base + 128], ref[0, pl.ds(r, BQ), base + 128:base + 256],
                ref[0, pl.ds(ws, tk), base + 256:base + 384], ref[0, pl.ds(ws, tk), base + 384:base + 512],
                _band_bias(r, ws, tk, D_BAND))

    def scatter(g, dil, c, r, out, lse):
        rows = pl.ds(r * dil + c, BQ, stride=dil) if dil > 1 else pl.ds(r, BQ)
        for slab in range(2):
            acc_ref[g, slab, rows, :] = out[slab * BQ:(slab + 1) * BQ]
            lse_ref[g, slab, rows, :] = lse[slab * BQ:(slab + 1) * BQ]

    def g0_step(i, carry):
        rs = [pl.multiple_of((i * BLOCKS_PER_STEP + j) * BQ, BQ) for j in range(BLOCKS_PER_STEP)]
        for r, (out, lse) in zip(rs, _banded_blocks([block(d0_ref, 1, 0, r) for r in rs])):
            scatter(0, 1, 0, r, out, lse)
        return carry

    lax.fori_loop(0, SEQ // (BQ * BLOCKS_PER_STEP), g0_step, 0)
    for c in range(4):
        rs = [j * BQ for j in range(SEQ // 4 // BQ)]
        for r, (out, lse) in zip(rs, _banded_blocks([block(d1_ref, 4, c, r) for r in rs])):
            scatter(1, 4, c, r, out, lse)
    for c0 in range(0, 16, BLOCKS_PER_STEP):
        cs = range(c0, c0 + BLOCKS_PER_STEP)
        for c, (out, lse) in zip(cs, _banded_blocks([block(d2_ref, 16, c, 0) for c in cs])):
            scatter(2, 16, c, 0, out, lse)

    def merge(i, carry):
        rows = pl.ds(pl.multiple_of(i * TQ, TQ), TQ)
        for slab in range(2):
            l0, l1, l2 = lse_ref[0, slab, rows, :], lse_ref[1, slab, rows, :], lse_ref[2, slab, rows, :]
            m = jnp.maximum(jnp.maximum(l0, l1), l2)
            w0, w1, w2 = jnp.exp2(l0 - m), jnp.exp2(l1 - m), jnp.exp2(l2 - m)
            num = w0 * acc_ref[0, slab, rows, :] + w1 * acc_ref[1, slab, rows, :] + w2 * acc_ref[2, slab, rows, :]
            o_ref[0, rows, slab * LANES:(slab + 1) * LANES] = (num * (1.0 / (w0 + w1 + w2))).astype(BF16)
        return carry

    lax.fori_loop(0, SEQ // TQ, merge, 0)


def _dilated_attention(views):
    bn = views[0].shape[0]
    return pl.pallas_call(
        _dilated_body,
        grid=(bn,),
        in_specs=[pl.BlockSpec((1,) + v.shape[1:], lambda b: (b, 0, 0), pipeline_mode=pl.Buffered(1)) for v in views],
        out_specs=pl.BlockSpec((1, SEQ, BRANCH_W), lambda b: (b, 0, 0)),
        out_shape=jax.ShapeDtypeStruct((bn, SEQ, BRANCH_W), BF16),
        scratch_shapes=[pltpu.VMEM((3, 2, SEQ, LANES), F32), pltpu.VMEM((3, 2, SEQ, LANES), F32)],
        compiler_params=_attn_params(1),
        name="attn_dilated",
    )(*views)


def _merge_body(x_ref, g_ref, ya_ref, yb_ref, yc_ref, yd_ref, wz_ref, wmg_ref, wb_ref, wo_ref, fg_ref, o_ref, *, final):
    x = x_ref[...]
    xn = _rms(x, g_ref[...]).astype(BF16)
    merged = jnp.zeros((TM, D_MODEL), F32)
    for i, y_ref in enumerate((ya_ref, yb_ref, yc_ref, yd_ref)):
        z = _dot(xn, wz_ref[:, i * BRANCH_W:(i + 1) * BRANCH_W])
        y = y_ref[...].astype(F32) * (z * _sigmoid(z))
        branch = _dot(y.astype(BF16), wb_ref[i])
        gate = _sigmoid(_dot(xn, wmg_ref[:, i * D_MODEL:(i + 1) * D_MODEL]))
        merged = merged + gate * branch
    out = x + _dot(merged.astype(BF16), wo_ref[...])
    if final:
        out = _rms(out, fg_ref[...])
    o_ref[...] = out


def _merge(x, ys, lw, final_g, final):
    n = x.shape[0]
    row = lambda w: pl.BlockSpec((TM, w), lambda i: (i, 0))
    return pl.pallas_call(
        functools.partial(_merge_body, final=final),
        grid=(n // TM,),
        in_specs=[
            row(D_MODEL),
            _const_spec((1, D_MODEL)),
            row(BRANCH_W), row(BRANCH_W), row(BRANCH_W), row(BRANCH_W),
            _const_spec((D_MODEL, N_BRANCH * BRANCH_W)),
            _const_spec((D_MODEL, N_BRANCH * D_MODEL)),
            _const_spec((N_BRANCH, BRANCH_W, D_MODEL)),
            _const_spec((D_MODEL, D_MODEL)),
            _const_spec((1, D_MODEL)),
        ],
        out_specs=row(D_MODEL),
        out_shape=jax.ShapeDtypeStruct((n, D_MODEL), F32),
        compiler_params=pltpu.CompilerParams(dimension_semantics=("parallel",), vmem_limit_bytes=VMEM_LIMIT),
        name="merge",
    )(x, lw["g_in"], *ys, lw["wz"], lw["wmg"], lw["wb"], lw["wo"], final_g)


def _gqa_cols(w):
    q = w[..., 0:256].reshape(w.shape[:-1] + (4, HEAD_DIM))[..., (0, 2, 1, 3), :].reshape(w.shape[:-1] + (256,))
    return jnp.concatenate([q, w[..., 256:512]], axis=-1)


def _slab_heads(w, axis):
    w = jnp.moveaxis(w, axis, -1)
    w = w.reshape(w.shape[:-1] + (4, HEAD_DIM))[..., (0, 2, 1, 3), :].reshape(w.shape[:-1] + (256,))
    return jnp.moveaxis(w, -1, axis)


def _layer_weights(l, norm_in, w_in, a_q_norm, w_q_up, a_kv_norm, w_kv_up, b_q_norm, b_k_norm, w_branch, w_out):
    w = w_in[l]
    zeros = lambda c: jnp.zeros((D_MODEL, c), F32)
    off_b = A_IN
    cols = [w[:, 0:384], zeros(64), w[:, 384:416], zeros(32)]
    for n in range(2 + len(D_GROUPS)):
        cols.append(_gqa_cols(w[:, off_b + n * QKV_W:off_b + (n + 1) * QKV_W]))
    w_attn = jnp.concatenate(cols, axis=1)

    wq = w_q_up[l].reshape(A_Q_LORA, A_HEADS, A_NOPE + A_ROPE)
    wq = jnp.pad(wq, ((0, 0), (0, 0), (0, LANES - A_NOPE - A_ROPE))).reshape(A_Q_LORA, A_HEADS * LANES)
    wkv = w_kv_up[l].reshape(A_KV_LORA, A_HEADS, A_NOPE + A_V)
    wk = jnp.pad(wkv[:, :, :A_NOPE], ((0, 0), (0, 0), (0, LANES - A_NOPE))).reshape(A_KV_LORA, A_HEADS * LANES)
    wv = wkv[:, :, A_NOPE:].reshape(A_KV_LORA, A_HEADS * A_V)

    wz = w[:, GATE_OFF:MERGE_OFF]
    wz = jnp.concatenate([wz[:, 0:256]] + [_slab_heads(wz[:, i * 256:(i + 1) * 256], 1) for i in (1, 2, 3)], axis=1)
    wb = jnp.stack([w_branch[l, 0]] + [_slab_heads(w_branch[l, i], 0) for i in (1, 2, 3)])

    head_id = jnp.arange(256) // HEAD_DIM
    return {
        "g_in": norm_in[l].reshape(1, D_MODEL),
        "w_attn": w_attn.astype(BF16),
        "wq": wq.astype(BF16),
        "wkv": jnp.concatenate([wk, wv], axis=1).astype(BF16),
        "gq": a_q_norm[l].reshape(1, A_Q_LORA),
        "gkv": a_kv_norm[l].reshape(1, A_KV_LORA),
        "gbq": jnp.tile(b_q_norm[l], 4).reshape(1, 256),
        "gbk": jnp.tile(b_k_norm[l], 2).reshape(1, 128),
        "gm": jnp.where(head_id[:, None] == head_id[None, :], 1.0 / HEAD_DIM, 0.0).astype(BF16),
        "wz": wz.astype(BF16),
        "wmg": w[:, MERGE_OFF:].astype(BF16),
        "wb": wb.astype(BF16),
        "wo": w_out[l].astype(BF16),
    }


def _rope_tables():
    pos = jnp.arange(SEQ)
    rows = jnp.repeat(jnp.arange(SEQ // GRID_W), GRID_W)
    cols = jnp.tile(jnp.arange(GRID_W), SEQ // GRID_W)

    def ang(p, d):
        freqs = ROPE_THETA ** (-jnp.arange(d // 2, dtype=F32) * 2.0 / d)
        return p.astype(F32)[:, None] * freqs[None, :]

    def pair(a):
        return jnp.concatenate([jnp.cos(a), jnp.cos(a)], axis=1), jnp.concatenate([-jnp.sin(a), jnp.sin(a)], axis=1)

    one, zero = jnp.ones((SEQ, 1), F32), jnp.zeros((SEQ, 1), F32)
    ca, sa = pair(ang(pos, A_ROPE))
    cos_a = jnp.concatenate([jnp.tile(one, (1, 64)), ca, jnp.tile(one, (1, 32))], axis=1)
    sin_a = jnp.concatenate([jnp.tile(zero, (1, 64)), sa, jnp.tile(zero, (1, 32))], axis=1)
    (cr, sr), (cc, sc) = pair(ang(rows, HEAD_DIM // 2)), pair(ang(cols, HEAD_DIM // 2))
    cos_b, sin_b = jnp.tile(jnp.concatenate([cr, cc], axis=1), (1, 2)), jnp.tile(jnp.concatenate([sr, sc], axis=1), (1, 2))
    cp, sp = pair(ang(pos, HEAD_DIM))
    return jnp.stack([cos_a, sin_a, cos_b, sin_b, jnp.tile(cp, (1, 2)), jnp.tile(sp, (1, 2))])


def _trunk_prepared(x, layers, tables, c_sink, final_g):
    bn = x.shape[0]
    x = x.reshape(bn * SEQ, D_MODEL)
    for l, lw in enumerate(layers):
        a, b, c, d0, d1, d2 = _project(x, lw, tables)
        seq = lambda t: t.reshape(bn, -1, t.shape[-1])
        ys = (
            _latent_attention(seq(a)),
            _axial_attention(seq(b)),
            _window_attention(seq(c), c_sink[l]),
            _dilated_attention([seq(d0), seq(d1), seq(d2)]),
        )
        ys = [y.reshape(bn * SEQ, BRANCH_W) for y in ys]
        x = _merge(x, ys, lw, final_g, final=(l == DEPTH - 1))
    return x.reshape(bn, SEQ, D_MODEL)


def _prepare(norm_in, w_in, a_q_norm, w_q_up, a_kv_norm, w_kv_up, b_q_norm, b_k_norm, c_sink, w_branch, w_out, final_norm):
    layers = [_layer_weights(l, norm_in, w_in, a_q_norm, w_q_up, a_kv_norm, w_kv_up, b_q_norm, b_k_norm, w_branch, w_out)
              for l in range(DEPTH)]
    return layers, _rope_tables(), c_sink, final_norm.reshape(1, D_MODEL)


def _trunk(x, *params):
    return _trunk_prepared(x, *_prepare(*params))


def kernel(x_prompt, x_sample, norm_in, w_in, a_q_norm, w_q_up, a_kv_norm, w_kv_up, b_q_norm, b_k_norm, c_sink, w_branch, w_out, final_norm):
    prepared = _prepare(norm_in, w_in, a_q_norm, w_q_up, a_kv_norm, w_kv_up, b_q_norm, b_k_norm, c_sink, w_branch, w_out, final_norm)
    return (_trunk_prepared(x_prompt, *prepared), _trunk_prepared(x_sample, *prepared))
```

```python
import functools

import jax
import jax.numpy as jnp
from jax import lax
from jax.experimental import pallas as pl
from jax.experimental.pallas import tpu as pltpu

D_MODEL = 1024
SEQ = 2048
DEPTH = 2
GRID_W = 64
ROPE_THETA = 10000.0
EPS = 1e-6
NEG = -1e30
HEAD_DIM = 64
LANES = 128
BRANCH_W = 256
N_BRANCH = 4
A_HEADS = 4
A_NOPE = 64
A_ROPE = 32
A_V = 64
A_Q_LORA = 256
A_KV_LORA = 128
A_IN = A_Q_LORA + A_KV_LORA + A_ROPE
QKV_W = 512
C_WINDOW = 128
D_GROUPS = ((128, 1), (512, 4), (2048, 16))
D_BAND = 64
GATE_OFF = A_IN + 2 * QKV_W + len(D_GROUPS) * QKV_W
MERGE_OFF = GATE_OFF + N_BRANCH * BRANCH_W
LOG2E = 1.4426950408889634
A_SCALE = (A_NOPE + A_ROPE) ** -0.5 * LOG2E
QK_SCALE = HEAD_DIM ** -0.5 * LOG2E

TM = 1024
TQ = 256
DENSE_TQ = 1024
DENSE_KC = 1024
VMEM_LIMIT = 56 * 1024 * 1024

F32 = jnp.float32
BF16 = jnp.bfloat16


def _dot(a, b):
    return jnp.dot(a, b, preferred_element_type=F32)


def _dot_nt(a, b):
    return lax.dot_general(a, b, (((1,), (1,)), ((), ())), preferred_element_type=F32)


def _rms(x, g):
    return x * lax.rsqrt(jnp.mean(x * x, axis=-1, keepdims=True) + EPS) * g


def _sigmoid(x):
    return 1.0 / (1.0 + jnp.exp(-x))


def _rope(a, cos, sin, shift):
    lane = lax.broadcasted_iota(jnp.int32, (1, LANES), 1)
    first = (lane % (2 * shift)) < shift
    outs = []
    for j in range(a.shape[1] // LANES):
        s = a[:, j * LANES:(j + 1) * LANES]
        partner = jnp.where(first, pltpu.roll(s, LANES - shift, 1), pltpu.roll(s, shift, 1))
        outs.append(s * cos + partner * sin)
    return outs[0] if len(outs) == 1 else jnp.concatenate(outs, axis=1)


def _head_mean_sq(x, gm):
    x2 = x * x
    hi = x2.astype(BF16)
    lo = (x2 - hi.astype(F32)).astype(BF16)
    return _dot(hi, gm) + _dot(lo, gm)


def _proj_body(x_ref, g_ref, w_ref, wq_ref, wkv_ref, gq_ref, gkv_ref, gbq_ref, gbk_ref, gm_ref, tab_ref,
               a_ref, b_ref, c_ref, d0_ref, d1_ref, d2_ref, stage_ref):
    xn = _rms(x_ref[...], g_ref[...]).astype(BF16)
    cos_a, sin_a = tab_ref[0], tab_ref[1]
    cos_b, sin_b = tab_ref[2], tab_ref[3]
    cos_c, sin_c = tab_ref[4], tab_ref[5]

    h = _dot(xn, w_ref[:, 0:512])
    q = _dot(_rms(h[:, 0:256], gq_ref[...]).astype(BF16), wq_ref[...])
    q = _rope(q, cos_a, sin_a, A_ROPE // 2) * A_SCALE
    kv = _dot(_rms(h[:, 256:384], gkv_ref[...]).astype(BF16), wkv_ref[...])
    kr = _rope(h[:, 384:512], cos_a, sin_a, A_ROPE // 2)
    k = kv[:, 0:512] + jnp.concatenate([kr] * A_HEADS, axis=1)
    a_ref[:, 0:512] = q.astype(BF16)
    a_ref[:, 512:1024] = k.astype(BF16)
    a_ref[:, 1024:1280] = kv[:, 512:768].astype(BF16)

    h = _dot(xn, w_ref[:, 512:1024])
    q, k = h[:, 0:256], h[:, 256:384]
    q = q * lax.rsqrt(_head_mean_sq(q, gm_ref[...]) + EPS) * gbq_ref[...]
    k = k * lax.rsqrt(_head_mean_sq(k, gm_ref[0:128, 0:128]) + EPS) * gbk_ref[...]
    b_ref[:, 0:256] = (_rope(q, cos_b, sin_b, HEAD_DIM // 4) * QK_SCALE).astype(BF16)
    b_ref[:, 256:384] = _rope(k, cos_b, sin_b, HEAD_DIM // 4).astype(BF16)
    b_ref[:, 384:512] = h[:, 384:512].astype(BF16)

    for n, (o_ref, dil) in enumerate(((c_ref, 1), (d0_ref, 1), (d1_ref, 4), (d2_ref, 16))):
        h = _dot(xn, w_ref[:, 1024 + n * QKV_W:1024 + (n + 1) * QKV_W])
        q = _rope(h[:, 0:256], cos_c, sin_c, HEAD_DIM // 2) * QK_SCALE
        k = _rope(h[:, 256:384], cos_c, sin_c, HEAD_DIM // 2)
        if dil == 1:
            o_ref[:, 0:256] = q.astype(BF16)
            o_ref[:, 256:384] = k.astype(BF16)
            o_ref[:, 384:512] = h[:, 384:512].astype(BF16)
        else:
            for s, slab in enumerate((q[:, 0:128], q[:, 128:256], k, h[:, 384:512])):
                stage_ref[s] = slab
            for c in range(dil):
                for s in range(QKV_W // LANES):
                    lanes = slice(c * QKV_W + s * LANES, c * QKV_W + (s + 1) * LANES)
                    o_ref[:, lanes] = stage_ref[s, pl.ds(c, TM // dil, stride=dil), :].astype(BF16)


def _const_spec(shape):
    return pl.BlockSpec(shape, lambda i: (0,) * len(shape), pipeline_mode=pl.Buffered(1))


def _project(x, lw, tables):
    n = x.shape[0]
    nt = SEQ // TM
    row = lambda w, dil=1: pl.BlockSpec((TM // dil, dil * w), lambda i: (i, 0))
    outs = ((1280, 1), (QKV_W, 1), (QKV_W, 1)) + tuple((QKV_W, dil) for _, dil in D_GROUPS)
    return pl.pallas_call(
        _proj_body,
        grid=(n // TM,),
        in_specs=[
            row(D_MODEL),
            _const_spec((1, D_MODEL)),
            _const_spec((D_MODEL, 3072)),
            _const_spec((A_Q_LORA, 512)),
            _const_spec((A_KV_LORA, 768)),
            _const_spec((1, A_Q_LORA)),
            _const_spec((1, A_KV_LORA)),
            _const_spec((1, 256)),
            _const_spec((1, 128)),
            _const_spec((256, 256)),
            pl.BlockSpec((6, TM, LANES), lambda i: (0, i % nt, 0)),
        ],
        out_specs=[row(w, dil) for w, dil in outs],
        out_shape=[jax.ShapeDtypeStruct((n // dil, dil * w), BF16) for w, dil in outs],
        scratch_shapes=[pltpu.VMEM((QKV_W // LANES, TM, LANES), F32)],
        compiler_params=pltpu.CompilerParams(dimension_semantics=("parallel",), vmem_limit_bytes=VMEM_LIMIT),
        name="proj",
    )(x, lw["g_in"], lw["w_attn"], lw["wq"], lw["wkv"], lw["gq"], lw["gkv"], lw["gbq"], lw["gbk"], lw["gm"], tables)


def _lane_lo():
    return lax.broadcasted_iota(jnp.int32, (1, LANES), 1) < HEAD_DIM


def _attend(q, k_ref, k_lanes, v_ref):
    m = l = acc = None
    for c in range(SEQ // DENSE_KC):
        keys = slice(c * DENSE_KC, (c + 1) * DENSE_KC)
        s = _dot_nt(q, k_ref[0, keys, k_lanes])
        m_c = jnp.max(s, axis=-1, keepdims=True)
        if m is None:
            m = m_c
            p = jnp.exp2(s - m)
            l = jnp.sum(p, axis=-1, keepdims=True)
            acc = _dot(p.astype(BF16), v_ref[keys])
        else:
            m_new = jnp.maximum(m, m_c)
            scale = jnp.exp2(m - m_new)
            p = jnp.exp2(s - m_new)
            l = scale * l + jnp.sum(p, axis=-1, keepdims=True)
            acc = scale * acc + _dot(p.astype(BF16), v_ref[keys])
            m = m_new
    return acc * (1.0 / l)


def _dense_body(q_ref, k_ref, v_ref, o_ref, vm_ref, *, shared):
    lo = _lane_lo()
    zero = jnp.zeros((), BF16)
    v = v_ref[0]
    vm_ref[0] = jnp.where(lo, v, zero)
    vm_ref[1] = jnp.where(lo, zero, v)

    for r in range(0, SEQ, DENSE_TQ):
        rows = slice(r, r + DENSE_TQ)
        if shared:
            qs = q_ref[0, rows, :]
            q_a, q_b = jnp.where(lo, qs, zero), jnp.where(lo, zero, qs)
            lanes_a = lanes_b = slice(0, LANES)
        else:
            q_a, q_b = q_ref[0, rows, 0:LANES], q_ref[0, rows, LANES:2 * LANES]
            lanes_a, lanes_b = slice(0, LANES), slice(LANES, 2 * LANES)
        out = _attend(q_a, k_ref, lanes_a, vm_ref.at[0]) + _attend(q_b, k_ref, lanes_b, vm_ref.at[1])
        o_ref[0, rows, :] = out.astype(BF16)


def _attn_params(n_grid):
    return pltpu.CompilerParams(dimension_semantics=("parallel",) * n_grid, vmem_limit_bytes=VMEM_LIMIT)


def _latent_attention(a):
    bn = a.shape[0]
    return pl.pallas_call(
        functools.partial(_dense_body, shared=False),
        grid=(bn, 2),
        in_specs=[
            pl.BlockSpec((1, SEQ, 256), lambda b, s: (b, 0, s)),
            pl.BlockSpec((1, SEQ, 256), lambda b, s: (b, 0, 2 + s)),
            pl.BlockSpec((1, SEQ, LANES), lambda b, s: (b, 0, 8 + s)),
        ],
        out_specs=pl.BlockSpec((1, SEQ, LANES), lambda b, s: (b, 0, s)),
        out_shape=jax.ShapeDtypeStruct((bn, SEQ, BRANCH_W), BF16),
        scratch_shapes=[pltpu.VMEM((2, SEQ, LANES), BF16)],
        compiler_params=_attn_params(2),
        name="attn_latent",
    )(a, a, a)


def _axial_attention(qkv):
    bn = qkv.shape[0]
    return pl.pallas_call(
        functools.partial(_dense_body, shared=True),
        grid=(bn, 2),
        in_specs=[
            pl.BlockSpec((1, SEQ, LANES), lambda b, s: (b, 0, s)),
            pl.BlockSpec((1, SEQ, LANES), lambda b, s: (b, 0, 2)),
            pl.BlockSpec((1, SEQ, LANES), lambda b, s: (b, 0, 3)),
        ],
        out_specs=pl.BlockSpec((1, SEQ, LANES), lambda b, s: (b, 0, s)),
        out_shape=jax.ShapeDtypeStruct((bn, SEQ, BRANCH_W), BF16),
        scratch_shapes=[pltpu.VMEM((2, SEQ, LANES), BF16)],
        compiler_params=_attn_params(2),
        name="attn_axial",
    )(qkv, qkv, qkv)


BQ = 128
BLOCKS_PER_STEP = 4


def _band_bias(q0, k0, tk, band):
    qpos = q0 + (lax.broadcasted_iota(jnp.int32, (2 * BQ, 1), 0) & (BQ - 1))
    kpos = k0 + lax.broadcasted_iota(jnp.int32, (1, tk), 1)
    return jnp.where(jnp.abs(qpos - kpos) <= band, 0.0, NEG).astype(F32)


def _banded_blocks(blocks, sinks=None):
    lo = _lane_lo()
    zero = jnp.zeros((), BF16)
    scores = []
    for q0, q1, k, _, bias in blocks:
        for first in (True, False):
            pick = (lambda x: jnp.where(lo, x, zero)) if first else (lambda x: jnp.where(lo, zero, x))
            scores.append(_dot_nt(jnp.concatenate([pick(q0), pick(q1)], axis=0), k) + bias)
    s = jnp.stack(scores)
    m = jnp.max(s, axis=-1, keepdims=True)
    p = jnp.exp2(s - m)
    l = jnp.sum(p, axis=-1, keepdims=True)
    if sinks is not None:
        l = l + jnp.exp2(sinks - m)
    p = p.astype(BF16)
    inv = 1.0 / l
    lse = m + jnp.log2(l)
    results = []
    for b, (_, _, _, v, _) in enumerate(blocks):
        o_lo = _dot(p[2 * b], jnp.where(lo, v, zero)) * inv[2 * b]
        o_hi = _dot(p[2 * b + 1], jnp.where(lo, zero, v)) * inv[2 * b + 1]
        results.append((o_lo + o_hi, jnp.where(lo, lse[2 * b], lse[2 * b + 1])))
    return results


C_KEYS = BQ + 2 * C_WINDOW


def _window_body(sink_ref, qkv_ref, o_ref):
    row = lax.broadcasted_iota(jnp.int32, (2 * BQ, 1), 0)
    sink_kv = [jnp.where(row < BQ, sink_ref[2 * h], sink_ref[2 * h + 1]) * LOG2E for h in range(2)]
    sinks = jnp.stack(sink_kv * BLOCKS_PER_STEP)

    def step(i, carry):
        blocks, rows = [], []
        for j in range(BLOCKS_PER_STEP):
            r = pl.multiple_of((i * BLOCKS_PER_STEP + j) * BQ, BQ)
            ws = pl.multiple_of(jnp.clip(r - C_WINDOW, 0, SEQ - C_KEYS), BQ)
            blocks.append((qkv_ref[0, pl.ds(r, BQ), 0:128], qkv_ref[0, pl.ds(r, BQ), 128:256],
                           qkv_ref[0, pl.ds(ws, C_KEYS), 256:384], qkv_ref[0, pl.ds(ws, C_KEYS), 384:512],
                           _band_bias(r, ws, C_KEYS, C_WINDOW)))
            rows.append(r)
        for r, (out, _) in zip(rows, _banded_blocks(blocks, sinks)):
            o_ref[0, pl.ds(r, BQ), 0:128] = out[0:BQ].astype(BF16)
            o_ref[0, pl.ds(r, BQ), 128:256] = out[BQ:2 * BQ].astype(BF16)
        return carry

    lax.fori_loop(0, SEQ // (BQ * BLOCKS_PER_STEP), step, 0)


def _window_attention(qkv, sink):
    bn = qkv.shape[0]
    return pl.pallas_call(
        _window_body,
        grid=(bn,),
        in_specs=[
            pl.BlockSpec(memory_space=pltpu.MemorySpace.SMEM),
            pl.BlockSpec((1, SEQ, QKV_W), lambda b: (b, 0, 0)),
        ],
        out_specs=pl.BlockSpec((1, SEQ, BRANCH_W), lambda b: (b, 0, 0)),
        out_shape=jax.ShapeDtypeStruct((bn, SEQ, BRANCH_W), BF16),
        compiler_params=_attn_params(1),
        name="attn_window",
    )(sink, qkv)


def _dilated_body(d0_ref, d1_ref, d2_ref, o_ref, acc_ref, lse_ref):
    def block(ref, dil, c, r):
        length = SEQ // dil
        tk = min(BQ + 2 * D_BAND, length)
        if isinstance(r, int):
            ws = min(max(r - D_BAND, 0), length - tk)
        else:
            ws = pl.multiple_of(jnp.clip(r - D_BAND, 0, length - tk), D_BAND)
        base = c * QKV_W
        return (ref[0, pl.ds(r, BQ), base:base + 128], ref[0, pl.ds(r, BQ), base + 128:base + 256],
                ref[0, pl.ds(ws, tk), base + 256:base + 384], ref[0, pl.ds(ws, tk), base + 384:base + 512],
                _band_bias(r, ws, tk, D_BAND))

    def scatter(g, dil, c, r, out, lse):
        rows = pl.ds(r * dil + c, BQ, stride=dil) if dil > 1 else pl.ds(r, BQ)
        for slab in range(2):
            acc_ref[g, slab, rows, :] = out[slab * BQ:(slab + 1) * BQ]
            lse_ref[g, slab, rows, :] = lse[slab * BQ:(slab + 1) * BQ]

    def g0_step(i, carry):
        rs = [pl.multiple_of((i * BLOCKS_PER_STEP + j) * BQ, BQ) for j in range(BLOCKS_PER_STEP)]
        for r, (out, lse) in zip(rs, _banded_blocks([block(d0_ref, 1, 0, r) for r in rs])):
            scatter(0, 1, 0, r, out, lse)
        return carry

    lax.fori_loop(0, SEQ // (BQ * BLOCKS_PER_STEP), g0_step, 0)
    for c in range(4):
        rs = [j * BQ for j in range(SEQ // 4 // BQ)]
        for r, (out, lse) in zip(rs, _banded_blocks([block(d1_ref, 4, c, r) for r in rs])):
            scatter(1, 4, c, r, out, lse)
    for c0 in range(0, 16, BLOCKS_PER_STEP):
        cs = range(c0, c0 + BLOCKS_PER_STEP)
        for c, (out, lse) in zip(cs, _banded_blocks([block(d2_ref, 16, c, 0) for c in cs])):
            scatter(2, 16, c, 0, out, lse)

    def merge(i, carry):
        rows = pl.ds(pl.multiple_of(i * TQ, TQ), TQ)
        for slab in range(2):
            l0, l1, l2 = lse_ref[0, slab, rows, :], lse_ref[1, slab, rows, :], lse_ref[2, slab, rows, :]
            m = jnp.maximum(jnp.maximum(l0, l1), l2)
            w0, w1, w2 = jnp.exp2(l0 - m), jnp.exp2(l1 - m), jnp.exp2(l2 - m)
            num = w0 * acc_ref[0, slab, rows, :] + w1 * acc_ref[1, slab, rows, :] + w2 * acc_ref[2, slab, rows, :]
            o_ref[0, rows, slab * LANES:(slab + 1) * LANES] = (num * (1.0 / (w0 + w1 + w2))).astype(BF16)
        return carry

    lax.fori_loop(0, SEQ // TQ, merge, 0)


def _dilated_attention(views):
    bn = views[0].shape[0]
    return pl.pallas_call(
        _dilated_body,
        grid=(bn,),
        in_specs=[pl.BlockSpec((1,) + v.shape[1:], lambda b: (b, 0, 0), pipeline_mode=pl.Buffered(1)) for v in views],
        out_specs=pl.BlockSpec((1, SEQ, BRANCH_W), lambda b: (b, 0, 0)),
        out_shape=jax.ShapeDtypeStruct((bn, SEQ, BRANCH_W), BF16),
        scratch_shapes=[pltpu.VMEM((3, 2, SEQ, LANES), F32), pltpu.VMEM((3, 2, SEQ, LANES), F32)],
        compiler_params=_attn_params(1),
        name="attn_dilated",
    )(*views)


def _merge_body(x_ref, g_ref, ya_ref, yb_ref, yc_ref, yd_ref, wz_ref, wmg_ref, wb_ref, wo_ref, fg_ref, o_ref, *, final):
    x = x_ref[...]
    xn = _rms(x, g_ref[...]).astype(BF16)
    merged = jnp.zeros((TM, D_MODEL), F32)
    for i, y_ref in enumerate((ya_ref, yb_ref, yc_ref, yd_ref)):
        z = _dot(xn, wz_ref[:, i * BRANCH_W:(i + 1) * BRANCH_W])
        y = y_ref[...].astype(F32) * (z * _sigmoid(z))
        branch = _dot(y.astype(BF16), wb_ref[i])
        gate = _sigmoid(_dot(xn, wmg_ref[:, i * D_MODEL:(i + 1) * D_MODEL]))
        merged = merged + gate * branch
    out = x + _dot(merged.astype(BF16), wo_ref[...])
    if final:
        out = _rms(out, fg_ref[...])
    o_ref[...] = out


def _merge(x, ys, lw, final_g, final):
    n = x.shape[0]
    row = lambda w: pl.BlockSpec((TM, w), lambda i: (i, 0))
    return pl.pallas_call(
        functools.partial(_merge_body, final=final),
        grid=(n // TM,),
        in_specs=[
            row(D_MODEL),
            _const_spec((1, D_MODEL)),
            row(BRANCH_W), row(BRANCH_W), row(BRANCH_W), row(BRANCH_W),
            _const_spec((D_MODEL, N_BRANCH * BRANCH_W)),
            _const_spec((D_MODEL, N_BRANCH * D_MODEL)),
            _const_spec((N_BRANCH, BRANCH_W, D_MODEL)),
            _const_spec((D_MODEL, D_MODEL)),
            _const_spec((1, D_MODEL)),
        ],
        out_specs=row(D_MODEL),
        out_shape=jax.ShapeDtypeStruct((n, D_MODEL), F32),
        compiler_params=pltpu.CompilerParams(dimension_semantics=("parallel",), vmem_limit_bytes=VMEM_LIMIT),
        name="merge",
    )(x, lw["g_in"], *ys, lw["wz"], lw["wmg"], lw["wb"], lw["wo"], final_g)


def _gqa_cols(w):
    q = w[..., 0:256].reshape(w.shape[:-1] + (4, HEAD_DIM))[..., (0, 2, 1, 3), :].reshape(w.shape[:-1] + (256,))
    return jnp.concatenate([q, w[..., 256:512]], axis=-1)


def _slab_heads(w, axis):
    w = jnp.moveaxis(w, axis, -1)
    w = w.reshape(w.shape[:-1] + (4, HEAD_DIM))[..., (0, 2, 1, 3), :].reshape(w.shape[:-1] + (256,))
    return jnp.moveaxis(w, -1, axis)


def _layer_weights(l, norm_in, w_in, a_q_norm, w_q_up, a_kv_norm, w_kv_up, b_q_norm, b_k_norm, w_branch, w_out):
    w = w_in[l]
    zeros = lambda c: jnp.zeros((D_MODEL, c), F32)
    off_b = A_IN
    cols = [w[:, 0:384], zeros(64), w[:, 384:416], zeros(32)]
    for n in range(2 + len(D_GROUPS)):
        cols.append(_gqa_cols(w[:, off_b + n * QKV_W:off_b + (n + 1) * QKV_W]))
    w_attn = jnp.concatenate(cols, axis=1)

    wq = w_q_up[l].reshape(A_Q_LORA, A_HEADS, A_NOPE + A_ROPE)
    wq = jnp.pad(wq, ((0, 0), (0, 0), (0, LANES - A_NOPE - A_ROPE))).reshape(A_Q_LORA, A_HEADS * LANES)
    wkv = w_kv_up[l].reshape(A_KV_LORA, A_HEADS, A_NOPE + A_V)
    wk = jnp.pad(wkv[:, :, :A_NOPE], ((0, 0), (0, 0), (0, LANES - A_NOPE))).reshape(A_KV_LORA, A_HEADS * LANES)
    wv = wkv[:, :, A_NOPE:].reshape(A_KV_LORA, A_HEADS * A_V)

    wz = w[:, GATE_OFF:MERGE_OFF]
    wz = jnp.concatenate([wz[:, 0:256]] + [_slab_heads(wz[:, i * 256:(i + 1) * 256], 1) for i in (1, 2, 3)], axis=1)
    wb = jnp.stack([w_branch[l, 0]] + [_slab_heads(w_branch[l, i], 0) for i in (1, 2, 3)])

    head_id = jnp.arange(256) // HEAD_DIM
    return {
        "g_in": norm_in[l].reshape(1, D_MODEL),
        "w_attn": w_attn.astype(BF16),
        "wq": wq.astype(BF16),
        "wkv": jnp.concatenate([wk, wv], axis=1).astype(BF16),
        "gq": a_q_norm[l].reshape(1, A_Q_LORA),
        "gkv": a_kv_norm[l].reshape(1, A_KV_LORA),
        "gbq": jnp.tile(b_q_norm[l], 4).reshape(1, 256),
        "gbk": jnp.tile(b_k_norm[l], 2).reshape(1, 128),
        "gm": jnp.where(head_id[:, None] == head_id[None, :], 1.0 / HEAD_DIM, 0.0).astype(BF16),
        "wz": wz.astype(BF16),
        "wmg": w[:, MERGE_OFF:].astype(BF16),
        "wb": wb.astype(BF16),
        "wo": w_out[l].astype(BF16),
    }


def _rope_tables():
    pos = jnp.arange(SEQ)
    rows = jnp.repeat(jnp.arange(SEQ // GRID_W), GRID_W)
    cols = jnp.tile(jnp.arange(GRID_W), SEQ // GRID_W)

    def ang(p, d):
        freqs = ROPE_THETA ** (-jnp.arange(d // 2, dtype=F32) * 2.0 / d)
        return p.astype(F32)[:, None] * freqs[None, :]

    def pair(a):
        return jnp.concatenate([jnp.cos(a), jnp.cos(a)], axis=1), jnp.concatenate([-jnp.sin(a), jnp.sin(a)], axis=1)

    one, zero = jnp.ones((SEQ, 1), F32), jnp.zeros((SEQ, 1), F32)
    ca, sa = pair(ang(pos, A_ROPE))
    cos_a = jnp.concatenate([jnp.tile(one, (1, 64)), ca, jnp.tile(one, (1, 32))], axis=1)
    sin_a = jnp.concatenate([jnp.tile(zero, (1, 64)), sa, jnp.tile(zero, (1, 32))], axis=1)
    (cr, sr), (cc, sc) = pair(ang(rows, HEAD_DIM // 2)), pair(ang(cols, HEAD_DIM // 2))
    cos_b, sin_b = jnp.tile(jnp.concatenate([cr, cc], axis=1), (1, 2)), jnp.tile(jnp.concatenate([sr, sc], axis=1), (1, 2))
    cp, sp = pair(ang(pos, HEAD_DIM))
    return jnp.stack([cos_a, sin_a, cos_b, sin_b, jnp.tile(cp, (1, 2)), jnp.tile(sp, (1, 2))])


def _trunk_prepared(x, layers, tables, c_sink, final_g):
    bn = x.shape[0]
    x = x.reshape(bn * SEQ, D_MODEL)
    for l, lw in enumerate(layers):
        a, b, c, d0, d1, d2 = _project(x, lw, tables)
        seq = lambda t: t.reshape(bn, -1, t.shape[-1])
        ys = (
            _latent_attention(seq(a)),
            _axial_attention(seq(b)),
            _window_attention(seq(c), c_sink[l]),
            _dilated_attention([seq(d0), seq(d1), seq(d2)]),
        )
        ys = [y.reshape(bn * SEQ, BRANCH_W) for y in ys]
        x = _merge(x, ys, lw, final_g, final=(l == DEPTH - 1))
    return x.reshape(bn, SEQ, D_MODEL)


def _prepare(norm_in, w_in, a_q_norm, w_q_up, a_kv_norm, w_kv_up, b_q_norm, b_k_norm, c_sink, w_branch, w_out, final_norm):
    layers = [_layer_weights(l, norm_in, w_in, a_q_norm, w_q_up, a_kv_norm, w_kv_up, b_q_norm, b_k_norm, w_branch, w_out)
              for l in range(DEPTH)]
    return layers, _rope_tables(), c_sink, final_norm.reshape(1, D_MODEL)


def _trunk(x, *params):
    return _trunk_prepared(x, *_prepare(*params))


def kernel(x_prompt, x_sample, norm_in, w_in, a_q_norm, w_q_up, a_kv_norm, w_kv_up, b_q_norm, b_k_norm, c_sink, w_branch, w_out, final_norm):
    prepared = _prepare(norm_in, w_in, a_q_norm, w_q_up, a_kv_norm, w_kv_up, b_q_norm, b_k_norm, c_sink, w_branch, w_out, final_norm)
    return (_trunk_prepared(x_prompt, *prepared), _trunk_prepared(x_sample, *prepared))
```

```python
import functools

import jax
import jax.numpy as jnp
from jax import lax
from jax.experimental import pallas as pl
from jax.experimental.pallas import tpu as pltpu

D_MODEL = 1024
SEQ = 2048
DEPTH = 2
GRID_W = 64
ROPE_THETA = 10000.0
EPS = 1e-6
NEG = -1e30
HEAD_DIM = 64
LANES = 128
BRANCH_W = 256
N_BRANCH = 4
A_HEADS = 4
A_NOPE = 64
A_ROPE = 32
A_V = 64
A_Q_LORA = 256
A_KV_LORA = 128
A_IN = A_Q_LORA + A_KV_LORA + A_ROPE
QKV_W = 512
C_WINDOW = 128
D_GROUPS = ((128, 1), (512, 4), (2048, 16))
D_BAND = 64
GATE_OFF = A_IN + 2 * QKV_W + len(D_GROUPS) * QKV_W
MERGE_OFF = GATE_OFF + N_BRANCH * BRANCH_W
LOG2E = 1.4426950408889634
A_SCALE = (A_NOPE + A_ROPE) ** -0.5 * LOG2E
QK_SCALE = HEAD_DIM ** -0.5 * LOG2E

TM = 1024
TQ = 256
DENSE_TQ = 1024
DENSE_KC = 1024
VMEM_LIMIT = 56 * 1024 * 1024

F32 = jnp.float32
BF16 = jnp.bfloat16


def _dot(a, b):
    return jnp.dot(a, b, preferred_element_type=F32)


def _dot_nt(a, b):
    return lax.dot_general(a, b, (((1,), (1,)), ((), ())), preferred_element_type=F32)


def _rms(x, g):
    return x * lax.rsqrt(jnp.mean(x * x, axis=-1, keepdims=True) + EPS) * g


def _sigmoid(x):
    return 1.0 / (1.0 + jnp.exp(-x))


def _rope(a, cos, sin, shift):
    lane = lax.broadcasted_iota(jnp.int32, (1, LANES), 1)
    first = (lane % (2 * shift)) < shift
    outs = []
    for j in range(a.shape[1] // LANES):
        s = a[:, j * LANES:(j + 1) * LANES]
        partner = jnp.where(first, pltpu.roll(s, LANES - shift, 1), pltpu.roll(s, shift, 1))
        outs.append(s * cos + partner * sin)
    return outs[0] if len(outs) == 1 else jnp.concatenate(outs, axis=1)


def _head_mean_sq(x, gm):
    x2 = x * x
    hi = x2.astype(BF16)
    lo = (x2 - hi.astype(F32)).astype(BF16)
    return _dot(hi, gm) + _dot(lo, gm)


def _proj_body(x_ref, g_ref, w_ref, wq_ref, wkv_ref, gq_ref, gkv_ref, gbq_ref, gbk_ref, gm_ref, tab_ref,
               a_ref, b_ref, c_ref, d0_ref, d1_ref, d2_ref, stage_ref):
    xn = _rms(x_ref[...], g_ref[...]).astype(BF16)
    cos_a, sin_a = tab_ref[0], tab_ref[1]
    cos_b, sin_b = tab_ref[2], tab_ref[3]
    cos_c, sin_c = tab_ref[4], tab_ref[5]

    h = _dot(xn, w_ref[:, 0:512])
    q = _dot(_rms(h[:, 0:256], gq_ref[...]).astype(BF16), wq_ref[...])
    q = _rope(q, cos_a, sin_a, A_ROPE // 2) * A_SCALE
    kv = _dot(_rms(h[:, 256:384], gkv_ref[...]).astype(BF16), wkv_ref[...])
    kr = _rope(h[:, 384:512], cos_a, sin_a, A_ROPE // 2)
    k = kv[:, 0:512] + jnp.concatenate([kr] * A_HEADS, axis=1)
    a_ref[:, 0:512] = q.astype(BF16)
    a_ref[:, 512:1024] = k.astype(BF16)
    a_ref[:, 1024:1280] = kv[:, 512:768].astype(BF16)

    h = _dot(xn, w_ref[:, 512:1024])
    q, k = h[:, 0:256], h[:, 256:384]
    q = q * lax.rsqrt(_head_mean_sq(q, gm_ref[...]) + EPS) * gbq_ref[...]
    k = k * lax.rsqrt(_head_mean_sq(k, gm_ref[0:128, 0:128]) + EPS) * gbk_ref[...]
    b_ref[:, 0:256] = (_rope(q, cos_b, sin_b, HEAD_DIM // 4) * QK_SCALE).astype(BF16)
    b_ref[:, 256:384] = _rope(k, cos_b, sin_b, HEAD_DIM // 4).astype(BF16)
    b_ref[:, 384:512] = h[:, 384:512].astype(BF16)

    for n, (o_ref, dil) in enumerate(((c_ref, 1), (d0_ref, 1), (d1_ref, 4), (d2_ref, 16))):
        h = _dot(xn, w_ref[:, 1024 + n * QKV_W:1024 + (n + 1) * QKV_W])
        q = _rope(h[:, 0:256], cos_c, sin_c, HEAD_DIM // 2) * QK_SCALE
        k = _rope(h[:, 256:384], cos_c, sin_c, HEAD_DIM // 2)
        if dil == 1:
            o_ref[:, 0:256] = q.astype(BF16)
            o_ref[:, 256:384] = k.astype(BF16)
            o_ref[:, 384:512] = h[:, 384:512].astype(BF16)
        else:
            for s, slab in enumerate((q[:, 0:128], q[:, 128:256], k, h[:, 384:512])):
                stage_ref[s] = slab
            for c in range(dil):
                for s in range(QKV_W // LANES):
                    lanes = slice(c * QKV_W + s * LANES, c * QKV_W + (s + 1) * LANES)
                    o_ref[:, lanes] = stage_ref[s, pl.ds(c, TM // dil, stride=dil), :].astype(BF16)


def _const_spec(shape):
    return pl.BlockSpec(shape, lambda i: (0,) * len(shape), pipeline_mode=pl.Buffered(1))


def _project(x, lw, tables):
    n = x.shape[0]
    nt = SEQ // TM
    row = lambda w, dil=1: pl.BlockSpec((TM // dil, dil * w), lambda i: (i, 0))
    outs = ((1280, 1), (QKV_W, 1), (QKV_W, 1)) + tuple((QKV_W, dil) for _, dil in D_GROUPS)
    return pl.pallas_call(
        _proj_body,
        grid=(n // TM,),
        in_specs=[
            row(D_MODEL),
            _const_spec((1, D_MODEL)),
            _const_spec((D_MODEL, 3072)),
            _const_spec((A_Q_LORA, 512)),
            _const_spec((A_KV_LORA, 768)),
            _const_spec((1, A_Q_LORA)),
            _const_spec((1, A_KV_LORA)),
            _const_spec((1, 256)),
            _const_spec((1, 128)),
            _const_spec((256, 256)),
            pl.BlockSpec((6, TM, LANES), lambda i: (0, i % nt, 0)),
        ],
        out_specs=[row(w, dil) for w, dil in outs],
        out_shape=[jax.ShapeDtypeStruct((n // dil, dil * w), BF16) for w, dil in outs],
        scratch_shapes=[pltpu.VMEM((QKV_W // LANES, TM, LANES), F32)],
        compiler_params=pltpu.CompilerParams(dimension_semantics=("parallel",), vmem_limit_bytes=VMEM_LIMIT),
        name="proj",
    )(x, lw["g_in"], lw["w_attn"], lw["wq"], lw["wkv"], lw["gq"], lw["gkv"], lw["gbq"], lw["gbk"], lw["gm"], tables)


def _lane_lo():
    return lax.broadcasted_iota(jnp.int32, (1, LANES), 1) < HEAD_DIM


def _attend(q, k_ref, k_lanes, v_ref):
    m = l = acc = None
    for c in range(SEQ // DENSE_KC):
        keys = slice(c * DENSE_KC, (c + 1) * DENSE_KC)
        s = _dot_nt(q, k_ref[0, keys, k_lanes])
        m_c = jnp.max(s, axis=-1, keepdims=True)
        if m is None:
            m = m_c
            p = jnp.exp2(s - m)
            l = jnp.sum(p, axis=-1, keepdims=True)
            acc = _dot(p.astype(BF16), v_ref[keys])
        else:
            m_new = jnp.maximum(m, m_c)
            scale = jnp.exp2(m - m_new)
            p = jnp.exp2(s - m_new)
            l = scale * l + jnp.sum(p, axis=-1, keepdims=True)
            acc = scale * acc + _dot(p.astype(BF16), v_ref[keys])
            m = m_new
    return acc * (1.0 / l)


def _dense_body(q_ref, k_ref, v_ref, o_ref, vm_ref, *, shared):
    lo = _lane_lo()
    zero = jnp.zeros((), BF16)
    v = v_ref[0]
    vm_ref[0] = jnp.where(lo, v, zero)
    vm_ref[1] = jnp.where(lo, zero, v)

    for r in range(0, SEQ, DENSE_TQ):
        rows = slice(r, r + DENSE_TQ)
        if shared:
            qs = q_ref[0, rows, :]
            q_a, q_b = jnp.where(lo, qs, zero), jnp.where(lo, zero, qs)
            lanes_a = lanes_b = slice(0, LANES)
        else:
            q_a, q_b = q_ref[0, rows, 0:LANES], q_ref[0, rows, LANES:2 * LANES]
            lanes_a, lanes_b = slice(0, LANES), slice(LANES, 2 * LANES)
        out = _attend(q_a, k_ref, lanes_a, vm_ref.at[0]) + _attend(q_b, k_ref, lanes_b, vm_ref.at[1])
        o_ref[0, rows, :] = out.astype(BF16)


def _attn_params(n_grid):
    return pltpu.CompilerParams(dimension_semantics=("parallel",) * n_grid, vmem_limit_bytes=VMEM_LIMIT)


def _latent_attention(a):
    bn = a.shape[0]
    return pl.pallas_call(
        functools.partial(_dense_body, shared=False),
        grid=(bn, 2),
        in_specs=[
            pl.BlockSpec((1, SEQ, 256), lambda b, s: (b, 0, s)),
            pl.BlockSpec((1, SEQ, 256), lambda b, s: (b, 0, 2 + s)),
            pl.BlockSpec((1, SEQ, LANES), lambda b, s: (b, 0, 8 + s)),
        ],
        out_specs=pl.BlockSpec((1, SEQ, LANES), lambda b, s: (b, 0, s)),
        out_shape=jax.ShapeDtypeStruct((bn, SEQ, BRANCH_W), BF16),
        scratch_shapes=[pltpu.VMEM((2, SEQ, LANES), BF16)],
        compiler_params=_attn_params(2),
        name="attn_latent",
    )(a, a, a)


def _axial_attention(qkv):
    bn = qkv.shape[0]
    return pl.pallas_call(
        functools.partial(_dense_body, shared=True),
        grid=(bn, 2),
        in_specs=[
            pl.BlockSpec((1, SEQ, LANES), lambda b, s: (b, 0, s)),
            pl.BlockSpec((1, SEQ, LANES), lambda b, s: (b, 0, 2)),
            pl.BlockSpec((1, SEQ, LANES), lambda b, s: (b, 0, 3)),
        ],
        out_specs=pl.BlockSpec((1, SEQ, LANES), lambda b, s: (b, 0, s)),
        out_shape=jax.ShapeDtypeStruct((bn, SEQ, BRANCH_W), BF16),
        scratch_shapes=[pltpu.VMEM((2, SEQ, LANES), BF16)],
        compiler_params=_attn_params(2),
        name="attn_axial",
    )(qkv, qkv, qkv)


BQ = 128
BLOCKS_PER_STEP = 8


def _band_window(r, length, tk, band):
    ws = min(max(r - band, 0), length - tk)
    return ws, (r - ws) // band


def _fill_band_bias(bias_ref, tk, band):
    qpos = lax.broadcasted_iota(jnp.int32, (2 * BQ, 1), 0) & (BQ - 1)
    kpos = lax.broadcasted_iota(jnp.int32, (1, tk), 1)
    for kind in range(3):
        bias_ref[kind] = jnp.where(jnp.abs(qpos + kind * band - kpos) <= band, 0.0, NEG).astype(F32)


def _banded_blocks(blocks, sinks=None):
    lo = _lane_lo()
    zero = jnp.zeros((), BF16)
    scores = []
    for q0, q1, k, _, bias in blocks:
        for first in (True, False):
            pick = (lambda x: jnp.where(lo, x, zero)) if first else (lambda x: jnp.where(lo, zero, x))
            scores.append(_dot_nt(jnp.concatenate([pick(q0), pick(q1)], axis=0), k) + bias)
    s = jnp.stack(scores)
    m = jnp.max(s, axis=-1, keepdims=True)
    p = jnp.exp2(s - m)
    l = jnp.sum(p, axis=-1, keepdims=True)
    if sinks is not None:
        l = l + jnp.exp2(sinks - m)
    p = p.astype(BF16)
    inv = 1.0 / l
    lse = m + jnp.log2(l)
    results = []
    for b, (_, _, _, v, _) in enumerate(blocks):
        o_lo = _dot(p[2 * b], jnp.where(lo, v, zero)) * inv[2 * b]
        o_hi = _dot(p[2 * b + 1], jnp.where(lo, zero, v)) * inv[2 * b + 1]
        results.append((o_lo + o_hi, jnp.where(lo, lse[2 * b], lse[2 * b + 1])))
    return results


C_KEYS = BQ + 2 * C_WINDOW
D_KEYS = BQ + 2 * D_BAND


def _window_body(sink_ref, qkv_ref, o_ref, bias_ref):
    _fill_band_bias(bias_ref, C_KEYS, C_WINDOW)
    row = lax.broadcasted_iota(jnp.int32, (2 * BQ, 1), 0)
    sink_kv = [jnp.where(row < BQ, sink_ref[2 * h], sink_ref[2 * h + 1]) * LOG2E for h in range(2)]
    sinks = jnp.stack(sink_kv * BLOCKS_PER_STEP)

    for r0 in range(0, SEQ, BQ * BLOCKS_PER_STEP):
        blocks, rows = [], []
        for r in range(r0, r0 + BQ * BLOCKS_PER_STEP, BQ):
            ws, kind = _band_window(r, SEQ, C_KEYS, C_WINDOW)
            blocks.append((qkv_ref[0, r:r + BQ, 0:128], qkv_ref[0, r:r + BQ, 128:256],
                           qkv_ref[0, ws:ws + C_KEYS, 256:384], qkv_ref[0, ws:ws + C_KEYS, 384:512], bias_ref[kind]))
            rows.append(r)
        for r, (out, _) in zip(rows, _banded_blocks(blocks, sinks)):
            o_ref[0, r:r + BQ, 0:128] = out[0:BQ].astype(BF16)
            o_ref[0, r:r + BQ, 128:256] = out[BQ:2 * BQ].astype(BF16)


def _window_attention(qkv, sink):
    bn = qkv.shape[0]
    return pl.pallas_call(
        _window_body,
        grid=(bn,),
        in_specs=[
            pl.BlockSpec(memory_space=pltpu.MemorySpace.SMEM),
            pl.BlockSpec((1, SEQ, QKV_W), lambda b: (b, 0, 0)),
        ],
        out_specs=pl.BlockSpec((1, SEQ, BRANCH_W), lambda b: (b, 0, 0)),
        out_shape=jax.ShapeDtypeStruct((bn, SEQ, BRANCH_W), BF16),
        scratch_shapes=[pltpu.VMEM((3, 2 * BQ, C_KEYS), F32)],
        compiler_params=_attn_params(1),
        name="attn_window",
    )(sink, qkv)


def _dilated_body(d0_ref, d1_ref, d2_ref, o_ref, acc_ref, lse_ref, bias_ref):
    _fill_band_bias(bias_ref, D_KEYS, D_BAND)

    def block(ref, dil, c, r):
        length = SEQ // dil
        tk = min(D_KEYS, length)
        ws, kind = _band_window(r, length, tk, D_BAND)
        base = c * QKV_W
        return (ref[0, r:r + BQ, base:base + 128], ref[0, r:r + BQ, base + 128:base + 256],
                ref[0, ws:ws + tk, base + 256:base + 384], ref[0, ws:ws + tk, base + 384:base + 512],
                bias_ref[kind, :, 0:tk])

    def scatter(g, dil, c, r, out, lse):
        rows = pl.ds(r * dil + c, BQ, stride=dil) if dil > 1 else pl.ds(r, BQ)
        for slab in range(2):
            acc_ref[g, slab, rows, :] = out[slab * BQ:(slab + 1) * BQ]
            lse_ref[g, slab, rows, :] = lse[slab * BQ:(slab + 1) * BQ]

    for g, (ref, (_, dil)) in enumerate(zip((d0_ref, d1_ref, d2_ref), D_GROUPS)):
        todo = [(c, r) for c in range(dil) for r in range(0, SEQ // dil, BQ)]
        for i in range(0, len(todo), BLOCKS_PER_STEP):
            batch = todo[i:i + BLOCKS_PER_STEP]
            for (c, r), (out, lse) in zip(batch, _banded_blocks([block(ref, dil, c, r) for c, r in batch])):
                scatter(g, dil, c, r, out, lse)

    def merge(i, carry):
        rows = pl.ds(pl.multiple_of(i * TQ, TQ), TQ)
        for slab in range(2):
            l0, l1, l2 = lse_ref[0, slab, rows, :], lse_ref[1, slab, rows, :], lse_ref[2, slab, rows, :]
            m = jnp.maximum(jnp.maximum(l0, l1), l2)
            w0, w1, w2 = jnp.exp2(l0 - m), jnp.exp2(l1 - m), jnp.exp2(l2 - m)
            num = w0 * acc_ref[0, slab, rows, :] + w1 * acc_ref[1, slab, rows, :] + w2 * acc_ref[2, slab, rows, :]
            o_ref[0, rows, slab * LANES:(slab + 1) * LANES] = (num * (1.0 / (w0 + w1 + w2))).astype(BF16)
        return carry

    lax.fori_loop(0, SEQ // TQ, merge, 0)


def _dilated_attention(views):
    bn = views[0].shape[0]
    return pl.pallas_call(
        _dilated_body,
        grid=(bn,),
        in_specs=[pl.BlockSpec((1,) + v.shape[1:], lambda b: (b, 0, 0), pipeline_mode=pl.Buffered(1)) for v in views],
        out_specs=pl.BlockSpec((1, SEQ, BRANCH_W), lambda b: (b, 0, 0)),
        out_shape=jax.ShapeDtypeStruct((bn, SEQ, BRANCH_W), BF16),
        scratch_shapes=[pltpu.VMEM((3, 2, SEQ, LANES), F32), pltpu.VMEM((3, 2, SEQ, LANES), F32),
                        pltpu.VMEM((3, 2 * BQ, D_KEYS), F32)],
        compiler_params=_attn_params(1),
        name="attn_dilated",
    )(*views)


def _merge_body(x_ref, g_ref, ya_ref, yb_ref, yc_ref, yd_ref, wz_ref, wmg_ref, wb_ref, wo_ref, fg_ref, o_ref, *, final):
    x = x_ref[...]
    xn = _rms(x, g_ref[...]).astype(BF16)
    merged = jnp.zeros((TM, D_MODEL), F32)
    for i, y_ref in enumerate((ya_ref, yb_ref, yc_ref, yd_ref)):
        z = _dot(xn, wz_ref[:, i * BRANCH_W:(i + 1) * BRANCH_W])
        y = y_ref[...].astype(F32) * (z * _sigmoid(z))
        branch = _dot(y.astype(BF16), wb_ref[i])
        gate = _sigmoid(_dot(xn, wmg_ref[:, i * D_MODEL:(i + 1) * D_MODEL]))
        merged = merged + gate * branch
    out = x + _dot(merged.astype(BF16), wo_ref[...])
    if final:
        out = _rms(out, fg_ref[...])
    o_ref[...] = out


def _merge(x, ys, lw, final_g, final):
    n = x.shape[0]
    row = lambda w: pl.BlockSpec((TM, w), lambda i: (i, 0))
    return pl.pallas_call(
        functools.partial(_merge_body, final=final),
        grid=(n // TM,),
        in_specs=[
            row(D_MODEL),
            _const_spec((1, D_MODEL)),
            row(BRANCH_W), row(BRANCH_W), row(BRANCH_W), row(BRANCH_W),
            _const_spec((D_MODEL, N_BRANCH * BRANCH_W)),
            _const_spec((D_MODEL, N_BRANCH * D_MODEL)),
            _const_spec((N_BRANCH, BRANCH_W, D_MODEL)),
            _const_spec((D_MODEL, D_MODEL)),
            _const_spec((1, D_MODEL)),
        ],
        out_specs=row(D_MODEL),
        out_shape=jax.ShapeDtypeStruct((n, D_MODEL), F32),
        compiler_params=pltpu.CompilerParams(dimension_semantics=("parallel",), vmem_limit_bytes=VMEM_LIMIT),
        name="merge",
    )(x, lw["g_in"], *ys, lw["wz"], lw["wmg"], lw["wb"], lw["wo"], final_g)


def _gqa_cols(w):
    q = w[..., 0:256].reshape(w.shape[:-1] + (4, HEAD_DIM))[..., (0, 2, 1, 3), :].reshape(w.shape[:-1] + (256,))
    return jnp.concatenate([q, w[..., 256:512]], axis=-1)


def _slab_heads(w, axis):
    w = jnp.moveaxis(w, axis, -1)
    w = w.reshape(w.shape[:-1] + (4, HEAD_DIM))[..., (0, 2, 1, 3), :].reshape(w.shape[:-1] + (256,))
    return jnp.moveaxis(w, -1, axis)


def _layer_weights(l, norm_in, w_in, a_q_norm, w_q_up, a_kv_norm, w_kv_up, b_q_norm, b_k_norm, w_branch, w_out):
    w = w_in[l]
    zeros = lambda c: jnp.zeros((D_MODEL, c), F32)
    off_b = A_IN
    cols = [w[:, 0:384], zeros(64), w[:, 384:416], zeros(32)]
    for n in range(2 + len(D_GROUPS)):
        cols.append(_gqa_cols(w[:, off_b + n * QKV_W:off_b + (n + 1) * QKV_W]))
    w_attn = jnp.concatenate(cols, axis=1)

    wq = w_q_up[l].reshape(A_Q_LORA, A_HEADS, A_NOPE + A_ROPE)
    wq = jnp.pad(wq, ((0, 0), (0, 0), (0, LANES - A_NOPE - A_ROPE))).reshape(A_Q_LORA, A_HEADS * LANES)
    wkv = w_kv_up[l].reshape(A_KV_LORA, A_HEADS, A_NOPE + A_V)
    wk = jnp.pad(wkv[:, :, :A_NOPE], ((0, 0), (0, 0), (0, LANES - A_NOPE))).reshape(A_KV_LORA, A_HEADS * LANES)
    wv = wkv[:, :, A_NOPE:].reshape(A_KV_LORA, A_HEADS * A_V)

    wz = w[:, GATE_OFF:MERGE_OFF]
    wz = jnp.concatenate([wz[:, 0:256]] + [_slab_heads(wz[:, i * 256:(i + 1) * 256], 1) for i in (1, 2, 3)], axis=1)
    wb = jnp.stack([w_branch[l, 0]] + [_slab_heads(w_branch[l, i], 0) for i in (1, 2, 3)])

    head_id = jnp.arange(256) // HEAD_DIM
    return {
        "g_in": norm_in[l].reshape(1, D_MODEL),
        "w_attn": w_attn.astype(BF16),
        "wq": wq.astype(BF16),
        "wkv": jnp.concatenate([wk, wv], axis=1).astype(BF16),
        "gq": a_q_norm[l].reshape(1, A_Q_LORA),
        "gkv": a_kv_norm[l].reshape(1, A_KV_LORA),
        "gbq": jnp.tile(b_q_norm[l], 4).reshape(1, 256),
        "gbk": jnp.tile(b_k_norm[l], 2).reshape(1, 128),
        "gm": jnp.where(head_id[:, None] == head_id[None, :], 1.0 / HEAD_DIM, 0.0).astype(BF16),
        "wz": wz.astype(BF16),
        "wmg": w[:, MERGE_OFF:].astype(BF16),
        "wb": wb.astype(BF16),
        "wo": w_out[l].astype(BF16),
    }


def _rope_tables():
    pos = jnp.arange(SEQ)
    rows = jnp.repeat(jnp.arange(SEQ // GRID_W), GRID_W)
    cols = jnp.tile(jnp.arange(GRID_W), SEQ // GRID_W)

    def ang(p, d):
        freqs = ROPE_THETA ** (-jnp.arange(d // 2, dtype=F32) * 2.0 / d)
        return p.astype(F32)[:, None] * freqs[None, :]

    def pair(a):
        return jnp.concatenate([jnp.cos(a), jnp.cos(a)], axis=1), jnp.concatenate([-jnp.sin(a), jnp.sin(a)], axis=1)

    one, zero = jnp.ones((SEQ, 1), F32), jnp.zeros((SEQ, 1), F32)
    ca, sa = pair(ang(pos, A_ROPE))
    cos_a = jnp.concatenate([jnp.tile(one, (1, 64)), ca, jnp.tile(one, (1, 32))], axis=1)
    sin_a = jnp.concatenate([jnp.tile(zero, (1, 64)), sa, jnp.tile(zero, (1, 32))], axis=1)
    (cr, sr), (cc, sc) = pair(ang(rows, HEAD_DIM // 2)), pair(ang(cols, HEAD_DIM // 2))
    cos_b, sin_b = jnp.tile(jnp.concatenate([cr, cc], axis=1), (1, 2)), jnp.tile(jnp.concatenate([sr, sc], axis=1), (1, 2))
    cp, sp = pair(ang(pos, HEAD_DIM))
    return jnp.stack([cos_a, sin_a, cos_b, sin_b, jnp.tile(cp, (1, 2)), jnp.tile(sp, (1, 2))])


def _trunk_prepared(x, layers, tables, c_sink, final_g):
    bn = x.shape[0]
    x = x.reshape(bn * SEQ, D_MODEL)
    for l, lw in enumerate(layers):
        a, b, c, d0, d1, d2 = _project(x, lw, tables)
        seq = lambda t: t.reshape(bn, -1, t.shape[-1])
        ys = (
            _latent_attention(seq(a)),
            _axial_attention(seq(b)),
            _window_attention(seq(c), c_sink[l]),
            _dilated_attention([seq(d0), seq(d1), seq(d2)]),
        )
        ys = [y.reshape(bn * SEQ, BRANCH_W) for y in ys]
        x = _merge(x, ys, lw, final_g, final=(l == DEPTH - 1))
    return x.reshape(bn, SEQ, D_MODEL)


def _prepare(norm_in, w_in, a_q_norm, w_q_up, a_kv_norm, w_kv_up, b_q_norm, b_k_norm, c_sink, w_branch, w_out, final_norm):
    layers = [_layer_weights(l, norm_in, w_in, a_q_norm, w_q_up, a_kv_norm, w_kv_up, b_q_norm, b_k_norm, w_branch, w_out)
              for l in range(DEPTH)]
    return layers, _rope_tables(), c_sink, final_norm.reshape(1, D_MODEL)


def _trunk(x, *params):
    return _trunk_prepared(x, *_prepare(*params))


def kernel(x_prompt, x_sample, norm_in, w_in, a_q_norm, w_q_up, a_kv_norm, w_kv_up, b_q_norm, b_k_norm, c_sink, w_branch, w_out, final_norm):
    prepared = _prepare(norm_in, w_in, a_q_norm, w_q_up, a_kv_norm, w_kv_up, b_q_norm, b_k_norm, c_sink, w_branch, w_out, final_norm)
    return (_trunk_prepared(x_prompt, *prepared), _trunk_prepared(x_sample, *prepared))
```

```python
import functools

import jax
import jax.numpy as jnp
from jax import lax
from jax.experimental import pallas as pl
from jax.experimental.pallas import tpu as pltpu

D_MODEL = 1024
SEQ = 2048
DEPTH = 2
GRID_W = 64
ROPE_THETA = 10000.0
EPS = 1e-6
NEG = -1e30
HEAD_DIM = 64
LANES = 128
BRANCH_W = 256
N_BRANCH = 4
A_HEADS = 4
A_NOPE = 64
A_ROPE = 32
A_V = 64
A_Q_LORA = 256
A_KV_LORA = 128
A_IN = A_Q_LORA + A_KV_LORA + A_ROPE
QKV_W = 512
C_WINDOW = 128
D_GROUPS = ((128, 1), (512, 4), (2048, 16))
D_BAND = 64
GATE_OFF = A_IN + 2 * QKV_W + len(D_GROUPS) * QKV_W
MERGE_OFF = GATE_OFF + N_BRANCH * BRANCH_W
LOG2E = 1.4426950408889634
A_SCALE = (A_NOPE + A_ROPE) ** -0.5 * LOG2E
QK_SCALE = HEAD_DIM ** -0.5 * LOG2E

TM = 1024
TQ = 256
DENSE_TQ = 1024
DENSE_KC = 1024
VMEM_LIMIT = 56 * 1024 * 1024

F32 = jnp.float32
BF16 = jnp.bfloat16


def _dot(a, b):
    return jnp.dot(a, b, preferred_element_type=F32)


def _dot_nt(a, b):
    return lax.dot_general(a, b, (((1,), (1,)), ((), ())), preferred_element_type=F32)


def _rms(x, g):
    return x * lax.rsqrt(jnp.mean(x * x, axis=-1, keepdims=True) + EPS) * g


def _sigmoid(x):
    return 1.0 / (1.0 + jnp.exp(-x))


def _rope(a, cos, sin, shift):
    lane = lax.broadcasted_iota(jnp.int32, (1, LANES), 1)
    first = (lane % (2 * shift)) < shift
    outs = []
    for j in range(a.shape[1] // LANES):
        s = a[:, j * LANES:(j + 1) * LANES]
        partner = jnp.where(first, pltpu.roll(s, LANES - shift, 1), pltpu.roll(s, shift, 1))
        outs.append(s * cos + partner * sin)
    return outs[0] if len(outs) == 1 else jnp.concatenate(outs, axis=1)


def _head_mean_sq(x, gm):
    x2 = x * x
    hi = x2.astype(BF16)
    lo = (x2 - hi.astype(F32)).astype(BF16)
    return _dot(hi, gm) + _dot(lo, gm)


def _proj_body(x_ref, g_ref, w_ref, wq_ref, wkv_ref, gq_ref, gkv_ref, gbq_ref, gbk_ref, gm_ref, tab_ref,
               a_ref, b_ref, c_ref, d0_ref, d1_ref, d2_ref, stage_ref):
    xn = _rms(x_ref[...], g_ref[...]).astype(BF16)
    cos_a, sin_a = tab_ref[0], tab_ref[1]
    cos_b, sin_b = tab_ref[2], tab_ref[3]
    cos_c, sin_c = tab_ref[4], tab_ref[5]

    h = _dot(xn, w_ref[:, 0:512])
    q = _dot(_rms(h[:, 0:256], gq_ref[...]).astype(BF16), wq_ref[...])
    q = _rope(q, cos_a, sin_a, A_ROPE // 2) * A_SCALE
    kv = _dot(_rms(h[:, 256:384], gkv_ref[...]).astype(BF16), wkv_ref[...])
    kr = _rope(h[:, 384:512], cos_a, sin_a, A_ROPE // 2)
    k = kv[:, 0:512] + jnp.concatenate([kr] * A_HEADS, axis=1)
    a_ref[:, 0:512] = q.astype(BF16)
    a_ref[:, 512:1024] = k.astype(BF16)
    a_ref[:, 1024:1280] = kv[:, 512:768].astype(BF16)

    h = _dot(xn, w_ref[:, 512:1024])
    q, k = h[:, 0:256], h[:, 256:384]
    q = q * lax.rsqrt(_head_mean_sq(q, gm_ref[...]) + EPS) * gbq_ref[...]
    k = k * lax.rsqrt(_head_mean_sq(k, gm_ref[0:128, 0:128]) + EPS) * gbk_ref[...]
    b_ref[:, 0:256] = (_rope(q, cos_b, sin_b, HEAD_DIM // 4) * QK_SCALE).astype(BF16)
    b_ref[:, 256:384] = _rope(k, cos_b, sin_b, HEAD_DIM // 4).astype(BF16)
    b_ref[:, 384:512] = h[:, 384:512].astype(BF16)

    for n, (o_ref, dil) in enumerate(((c_ref, 1), (d0_ref, 1), (d1_ref, 4), (d2_ref, 16))):
        h = _dot(xn, w_ref[:, 1024 + n * QKV_W:1024 + (n + 1) * QKV_W])
        q = _rope(h[:, 0:256], cos_c, sin_c, HEAD_DIM // 2) * QK_SCALE
        k = _rope(h[:, 256:384], cos_c, sin_c, HEAD_DIM // 2)
        if dil == 1:
            o_ref[:, 0:256] = q.astype(BF16)
            o_ref[:, 256:384] = k.astype(BF16)
            o_ref[:, 384:512] = h[:, 384:512].astype(BF16)
        else:
            for s, slab in enumerate((q[:, 0:128], q[:, 128:256], k, h[:, 384:512])):
                stage_ref[s] = slab
            for c in range(dil):
                for s in range(QKV_W // LANES):
                    lanes = slice(c * QKV_W + s * LANES, c * QKV_W + (s + 1) * LANES)
                    o_ref[:, lanes] = stage_ref[s, pl.ds(c, TM // dil, stride=dil), :].astype(BF16)


def _const_spec(shape):
    return pl.BlockSpec(shape, lambda i: (0,) * len(shape), pipeline_mode=pl.Buffered(1))


def _project(x, lw, tables):
    n = x.shape[0]
    nt = SEQ // TM
    row = lambda w, dil=1: pl.BlockSpec((TM // dil, dil * w), lambda i: (i, 0))
    outs = ((1280, 1), (QKV_W, 1), (QKV_W, 1)) + tuple((QKV_W, dil) for _, dil in D_GROUPS)
    return pl.pallas_call(
        _proj_body,
        grid=(n // TM,),
        in_specs=[
            row(D_MODEL),
            _const_spec((1, D_MODEL)),
            _const_spec((D_MODEL, 3072)),
            _const_spec((A_Q_LORA, 512)),
            _const_spec((A_KV_LORA, 768)),
            _const_spec((1, A_Q_LORA)),
            _const_spec((1, A_KV_LORA)),
            _const_spec((1, 256)),
            _const_spec((1, 128)),
            _const_spec((256, 256)),
            pl.BlockSpec((6, TM, LANES), lambda i: (0, i % nt, 0)),
        ],
        out_specs=[row(w, dil) for w, dil in outs],
        out_shape=[jax.ShapeDtypeStruct((n // dil, dil * w), BF16) for w, dil in outs],
        scratch_shapes=[pltpu.VMEM((QKV_W // LANES, TM, LANES), F32)],
        compiler_params=pltpu.CompilerParams(dimension_semantics=("parallel",), vmem_limit_bytes=VMEM_LIMIT),
        name="proj",
    )(x, lw["g_in"], lw["w_attn"], lw["wq"], lw["wkv"], lw["gq"], lw["gkv"], lw["gbq"], lw["gbk"], lw["gm"], tables)


def _lane_lo():
    return lax.broadcasted_iota(jnp.int32, (1, LANES), 1) < HEAD_DIM


def _attend(q, k_ref, k_lanes, v_ref):
    m = l = acc = None
    for c in range(SEQ // DENSE_KC):
        keys = slice(c * DENSE_KC, (c + 1) * DENSE_KC)
        s = _dot_nt(q, k_ref[0, keys, k_lanes])
        m_c = jnp.max(s, axis=-1, keepdims=True)
        if m is None:
            m = m_c
            p = jnp.exp2(s - m)
            l = jnp.sum(p, axis=-1, keepdims=True)
            acc = _dot(p.astype(BF16), v_ref[keys])
        else:
            m_new = jnp.maximum(m, m_c)
            scale = jnp.exp2(m - m_new)
            p = jnp.exp2(s - m_new)
            l = scale * l + jnp.sum(p, axis=-1, keepdims=True)
            acc = scale * acc + _dot(p.astype(BF16), v_ref[keys])
            m = m_new
    return acc * (1.0 / l)


def _dense_body(q_ref, k_ref, v_ref, o_ref, vm_ref, *, shared):
    lo = _lane_lo()
    zero = jnp.zeros((), BF16)
    v = v_ref[0]
    vm_ref[0] = jnp.where(lo, v, zero)
    vm_ref[1] = jnp.where(lo, zero, v)

    for r in range(0, SEQ, DENSE_TQ):
        rows = slice(r, r + DENSE_TQ)
        if shared:
            qs = q_ref[0, rows, :]
            q_a, q_b = jnp.where(lo, qs, zero), jnp.where(lo, zero, qs)
            lanes_a = lanes_b = slice(0, LANES)
        else:
            q_a, q_b = q_ref[0, rows, 0:LANES], q_ref[0, rows, LANES:2 * LANES]
            lanes_a, lanes_b = slice(0, LANES), slice(LANES, 2 * LANES)
        out = _attend(q_a, k_ref, lanes_a, vm_ref.at[0]) + _attend(q_b, k_ref, lanes_b, vm_ref.at[1])
        o_ref[0, rows, :] = out.astype(BF16)


def _attn_params(n_grid):
    return pltpu.CompilerParams(dimension_semantics=("parallel",) * n_grid, vmem_limit_bytes=VMEM_LIMIT)


def _latent_attention(a):
    bn = a.shape[0]
    return pl.pallas_call(
        functools.partial(_dense_body, shared=False),
        grid=(bn, 2),
        in_specs=[
            pl.BlockSpec((1, SEQ, 256), lambda b, s: (b, 0, s)),
            pl.BlockSpec((1, SEQ, 256), lambda b, s: (b, 0, 2 + s)),
            pl.BlockSpec((1, SEQ, LANES), lambda b, s: (b, 0, 8 + s)),
        ],
        out_specs=pl.BlockSpec((1, SEQ, LANES), lambda b, s: (b, 0, s)),
        out_shape=jax.ShapeDtypeStruct((bn, SEQ, BRANCH_W), BF16),
        scratch_shapes=[pltpu.VMEM((2, SEQ, LANES), BF16)],
        compiler_params=_attn_params(2),
        name="attn_latent",
    )(a, a, a)


def _axial_attention(qkv):
    bn = qkv.shape[0]
    return pl.pallas_call(
        functools.partial(_dense_body, shared=True),
        grid=(bn, 2),
        in_specs=[
            pl.BlockSpec((1, SEQ, LANES), lambda b, s: (b, 0, s)),
            pl.BlockSpec((1, SEQ, LANES), lambda b, s: (b, 0, 2)),
            pl.BlockSpec((1, SEQ, LANES), lambda b, s: (b, 0, 3)),
        ],
        out_specs=pl.BlockSpec((1, SEQ, LANES), lambda b, s: (b, 0, s)),
        out_shape=jax.ShapeDtypeStruct((bn, SEQ, BRANCH_W), BF16),
        scratch_shapes=[pltpu.VMEM((2, SEQ, LANES), BF16)],
        compiler_params=_attn_params(2),
        name="attn_axial",
    )(qkv, qkv, qkv)


BQ = 128
BLOCKS_PER_STEP = 8


def _band_window(r, length, tk, band):
    ws = min(max(r - band, 0), length - tk)
    return ws, (r - ws) // band


def _fill_band_bias(bias_ref, tk, band):
    kpos = lax.broadcasted_iota(jnp.int32, (tk, 1), 0)
    qpos = lax.broadcasted_iota(jnp.int32, (1, 2 * BQ), 1) & (BQ - 1)
    for kind in range(3):
        bias_ref[kind] = jnp.where(jnp.abs(qpos + kind * band - kpos) <= band, 0.0, NEG).astype(F32)


def _banded_blocks(blocks, sinks=None, want_lse=False):
    lo = _lane_lo()
    zero = jnp.zeros((), BF16)
    kv0_rows = lax.broadcasted_iota(jnp.int32, (LANES, 1), 0) < HEAD_DIM
    scores = []
    for q0, q1, k, _, bias in blocks:
        qq = jnp.concatenate([q0, q1], axis=0)
        scores.append(_dot_nt(jnp.where(lo, k, zero), qq) + bias)
        scores.append(_dot_nt(jnp.where(lo, zero, k), qq) + bias)
    s = jnp.stack(scores)
    m = jnp.max(s, axis=1, keepdims=True)
    p = jnp.exp2(s - m)
    l = jnp.sum(p, axis=1, keepdims=True)
    if sinks is not None:
        l = l + jnp.exp2(sinks - m)
    p = p.astype(BF16)
    inv = 1.0 / l
    lse = m + jnp.log2(l) if want_lse else None
    results = []
    for b, (_, _, _, v, _) in enumerate(blocks):
        o_lo = lax.dot_general(v, p[2 * b], (((0,), (0,)), ((), ())), preferred_element_type=F32) * inv[2 * b]
        o_hi = lax.dot_general(v, p[2 * b + 1], (((0,), (0,)), ((), ())), preferred_element_type=F32) * inv[2 * b + 1]
        out = jnp.where(kv0_rows, o_lo, o_hi).T
        results.append((out, jnp.where(kv0_rows, lse[2 * b], lse[2 * b + 1]).T if want_lse else None))
    return results


C_KEYS = BQ + 2 * C_WINDOW
D_KEYS = BQ + 2 * D_BAND


def _window_body(sink_ref, qkv_ref, o_ref, bias_ref):
    _fill_band_bias(bias_ref, C_KEYS, C_WINDOW)
    lane = lax.broadcasted_iota(jnp.int32, (1, 2 * BQ), 1)
    sink_kv = [jnp.where(lane < BQ, sink_ref[2 * h], sink_ref[2 * h + 1]) * LOG2E for h in range(2)]
    sinks = jnp.stack(sink_kv * BLOCKS_PER_STEP)

    for r0 in range(0, SEQ, BQ * BLOCKS_PER_STEP):
        blocks, rows = [], []
        for r in range(r0, r0 + BQ * BLOCKS_PER_STEP, BQ):
            ws, kind = _band_window(r, SEQ, C_KEYS, C_WINDOW)
            blocks.append((qkv_ref[0, r:r + BQ, 0:128], qkv_ref[0, r:r + BQ, 128:256],
                           qkv_ref[0, ws:ws + C_KEYS, 256:384], qkv_ref[0, ws:ws + C_KEYS, 384:512], bias_ref[kind]))
            rows.append(r)
        for r, (out, _) in zip(rows, _banded_blocks(blocks, sinks)):
            o_ref[0, r:r + BQ, 0:128] = out[0:BQ].astype(BF16)
            o_ref[0, r:r + BQ, 128:256] = out[BQ:2 * BQ].astype(BF16)


def _window_attention(qkv, sink):
    bn = qkv.shape[0]
    return pl.pallas_call(
        _window_body,
        grid=(bn,),
        in_specs=[
            pl.BlockSpec(memory_space=pltpu.MemorySpace.SMEM),
            pl.BlockSpec((1, SEQ, QKV_W), lambda b: (b, 0, 0)),
        ],
        out_specs=pl.BlockSpec((1, SEQ, BRANCH_W), lambda b: (b, 0, 0)),
        out_shape=jax.ShapeDtypeStruct((bn, SEQ, BRANCH_W), BF16),
        scratch_shapes=[pltpu.VMEM((3, C_KEYS, 2 * BQ), F32)],
        compiler_params=_attn_params(1),
        name="attn_window",
    )(sink, qkv)


def _dilated_body(d0_ref, d1_ref, d2_ref, o_ref, acc_ref, lse_ref, bias_ref):
    _fill_band_bias(bias_ref, D_KEYS, D_BAND)

    def block(ref, dil, c, r):
        length = SEQ // dil
        tk = min(D_KEYS, length)
        ws, kind = _band_window(r, length, tk, D_BAND)
        base = c * QKV_W
        return (ref[0, r:r + BQ, base:base + 128], ref[0, r:r + BQ, base + 128:base + 256],
                ref[0, ws:ws + tk, base + 256:base + 384], ref[0, ws:ws + tk, base + 384:base + 512],
                bias_ref[kind, 0:tk, :])

    def scatter(g, dil, c, r, out, lse):
        rows = pl.ds(r * dil + c, BQ, stride=dil) if dil > 1 else pl.ds(r, BQ)
        for slab in range(2):
            acc_ref[g, slab, rows, :] = out[slab * BQ:(slab + 1) * BQ]
            lse_ref[g, slab, rows, :] = lse[slab * BQ:(slab + 1) * BQ]

    for g, (ref, (_, dil)) in enumerate(zip((d0_ref, d1_ref, d2_ref), D_GROUPS)):
        todo = [(c, r) for c in range(dil) for r in range(0, SEQ // dil, BQ)]
        for i in range(0, len(todo), BLOCKS_PER_STEP):
            batch = todo[i:i + BLOCKS_PER_STEP]
            for (c, r), (out, lse) in zip(batch, _banded_blocks([block(ref, dil, c, r) for c, r in batch], want_lse=True)):
                scatter(g, dil, c, r, out, lse)

    def merge(i, carry):
        rows = pl.ds(pl.multiple_of(i * TQ, TQ), TQ)
        for slab in range(2):
            l0, l1, l2 = lse_ref[0, slab, rows, :], lse_ref[1, slab, rows, :], lse_ref[2, slab, rows, :]
            m = jnp.maximum(jnp.maximum(l0, l1), l2)
            w0, w1, w2 = jnp.exp2(l0 - m), jnp.exp2(l1 - m), jnp.exp2(l2 - m)
            num = w0 * acc_ref[0, slab, rows, :] + w1 * acc_ref[1, slab, rows, :] + w2 * acc_ref[2, slab, rows, :]
            o_ref[0, rows, slab * LANES:(slab + 1) * LANES] = (num * (1.0 / (w0 + w1 + w2))).astype(BF16)
        return carry

    lax.fori_loop(0, SEQ // TQ, merge, 0)


def _dilated_attention(views):
    bn = views[0].shape[0]
    return pl.pallas_call(
        _dilated_body,
        grid=(bn,),
        in_specs=[pl.BlockSpec((1,) + v.shape[1:], lambda b: (b, 0, 0), pipeline_mode=pl.Buffered(1)) for v in views],
        out_specs=pl.BlockSpec((1, SEQ, BRANCH_W), lambda b: (b, 0, 0)),
        out_shape=jax.ShapeDtypeStruct((bn, SEQ, BRANCH_W), BF16),
        scratch_shapes=[pltpu.VMEM((3, 2, SEQ, LANES), F32), pltpu.VMEM((3, 2, SEQ, LANES), F32),
                        pltpu.VMEM((3, D_KEYS, 2 * BQ), F32)],
        compiler_params=_attn_params(1),
        name="attn_dilated",
    )(*views)


def _merge_body(x_ref, g_ref, ya_ref, yb_ref, yc_ref, yd_ref, wz_ref, wmg_ref, wb_ref, wo_ref, fg_ref, o_ref, *, final):
    x = x_ref[...]
    xn = _rms(x, g_ref[...]).astype(BF16)
    merged = jnp.zeros((TM, D_MODEL), F32)
    for i, y_ref in enumerate((ya_ref, yb_ref, yc_ref, yd_ref)):
        z = _dot(xn, wz_ref[:, i * BRANCH_W:(i + 1) * BRANCH_W])
        y = y_ref[...].astype(F32) * (z * _sigmoid(z))
        branch = _dot(y.astype(BF16), wb_ref[i])
        gate = _sigmoid(_dot(xn, wmg_ref[:, i * D_MODEL:(i + 1) * D_MODEL]))
        merged = merged + gate * branch
    out = x + _dot(merged.astype(BF16), wo_ref[...])
    if final:
        out = _rms(out, fg_ref[...])
    o_ref[...] = out


def _merge(x, ys, lw, final_g, final):
    n = x.shape[0]
    row = lambda w: pl.BlockSpec((TM, w), lambda i: (i, 0))
    return pl.pallas_call(
        functools.partial(_merge_body, final=final),
        grid=(n // TM,),
        in_specs=[
            row(D_MODEL),
            _const_spec((1, D_MODEL)),
            row(BRANCH_W), row(BRANCH_W), row(BRANCH_W), row(BRANCH_W),
            _const_spec((D_MODEL, N_BRANCH * BRANCH_W)),
            _const_spec((D_MODEL, N_BRANCH * D_MODEL)),
            _const_spec((N_BRANCH, BRANCH_W, D_MODEL)),
            _const_spec((D_MODEL, D_MODEL)),
            _const_spec((1, D_MODEL)),
        ],
        out_specs=row(D_MODEL),
        out_shape=jax.ShapeDtypeStruct((n, D_MODEL), F32),
        compiler_params=pltpu.CompilerParams(dimension_semantics=("parallel",), vmem_limit_bytes=VMEM_LIMIT),
        name="merge",
    )(x, lw["g_in"], *ys, lw["wz"], lw["wmg"], lw["wb"], lw["wo"], final_g)


def _gqa_cols(w):
    q = w[..., 0:256].reshape(w.shape[:-1] + (4, HEAD_DIM))[..., (0, 2, 1, 3), :].reshape(w.shape[:-1] + (256,))
    return jnp.concatenate([q, w[..., 256:512]], axis=-1)


def _slab_heads(w, axis):
    w = jnp.moveaxis(w, axis, -1)
    w = w.reshape(w.shape[:-1] + (4, HEAD_DIM))[..., (0, 2, 1, 3), :].reshape(w.shape[:-1] + (256,))
    return jnp.moveaxis(w, -1, axis)


def _layer_weights(l, norm_in, w_in, a_q_norm, w_q_up, a_kv_norm, w_kv_up, b_q_norm, b_k_norm, w_branch, w_out):
    w = w_in[l]
    zeros = lambda c: jnp.zeros((D_MODEL, c), F32)
    off_b = A_IN
    cols = [w[:, 0:384], zeros(64), w[:, 384:416], zeros(32)]
    for n in range(2 + len(D_GROUPS)):
        cols.append(_gqa_cols(w[:, off_b + n * QKV_W:off_b + (n + 1) * QKV_W]))
    w_attn = jnp.concatenate(cols, axis=1)

    wq = w_q_up[l].reshape(A_Q_LORA, A_HEADS, A_NOPE + A_ROPE)
    wq = jnp.pad(wq, ((0, 0), (0, 0), (0, LANES - A_NOPE - A_ROPE))).reshape(A_Q_LORA, A_HEADS * LANES)
    wkv = w_kv_up[l].reshape(A_KV_LORA, A_HEADS, A_NOPE + A_V)
    wk = jnp.pad(wkv[:, :, :A_NOPE], ((0, 0), (0, 0), (0, LANES - A_NOPE))).reshape(A_KV_LORA, A_HEADS * LANES)
    wv = wkv[:, :, A_NOPE:].reshape(A_KV_LORA, A_HEADS * A_V)

    wz = w[:, GATE_OFF:MERGE_OFF]
    wz = jnp.concatenate([wz[:, 0:256]] + [_slab_heads(wz[:, i * 256:(i + 1) * 256], 1) for i in (1, 2, 3)], axis=1)
    wb = jnp.stack([w_branch[l, 0]] + [_slab_heads(w_branch[l, i], 0) for i in (1, 2, 3)])

    head_id = jnp.arange(256) // HEAD_DIM
    return {
        "g_in": norm_in[l].reshape(1, D_MODEL),
        "w_attn": w_attn.astype(BF16),
        "wq": wq.astype(BF16),
        "wkv": jnp.concatenate([wk, wv], axis=1).astype(BF16),
        "gq": a_q_norm[l].reshape(1, A_Q_LORA),
        "gkv": a_kv_norm[l].reshape(1, A_KV_LORA),
        "gbq": jnp.tile(b_q_norm[l], 4).reshape(1, 256),
        "gbk": jnp.tile(b_k_norm[l], 2).reshape(1, 128),
        "gm": jnp.where(head_id[:, None] == head_id[None, :], 1.0 / HEAD_DIM, 0.0).astype(BF16),
        "wz": wz.astype(BF16),
        "wmg": w[:, MERGE_OFF:].astype(BF16),
        "wb": wb.astype(BF16),
        "wo": w_out[l].astype(BF16),
    }


def _rope_tables():
    pos = jnp.arange(SEQ)
    rows = jnp.repeat(jnp.arange(SEQ // GRID_W), GRID_W)
    cols = jnp.tile(jnp.arange(GRID_W), SEQ // GRID_W)

    def ang(p, d):
        freqs = ROPE_THETA ** (-jnp.arange(d // 2, dtype=F32) * 2.0 / d)
        return p.astype(F32)[:, None] * freqs[None, :]

    def pair(a):
        return jnp.concatenate([jnp.cos(a), jnp.cos(a)], axis=1), jnp.concatenate([-jnp.sin(a), jnp.sin(a)], axis=1)

    one, zero = jnp.ones((SEQ, 1), F32), jnp.zeros((SEQ, 1), F32)
    ca, sa = pair(ang(pos, A_ROPE))
    cos_a = jnp.concatenate([jnp.tile(one, (1, 64)), ca, jnp.tile(one, (1, 32))], axis=1)
    sin_a = jnp.concatenate([jnp.tile(zero, (1, 64)), sa, jnp.tile(zero, (1, 32))], axis=1)
    (cr, sr), (cc, sc) = pair(ang(rows, HEAD_DIM // 2)), pair(ang(cols, HEAD_DIM // 2))
    cos_b, sin_b = jnp.tile(jnp.concatenate([cr, cc], axis=1), (1, 2)), jnp.tile(jnp.concatenate([sr, sc], axis=1), (1, 2))
    cp, sp = pair(ang(pos, HEAD_DIM))
    return jnp.stack([cos_a, sin_a, cos_b, sin_b, jnp.tile(cp, (1, 2)), jnp.tile(sp, (1, 2))])


def _trunk_prepared(x, layers, tables, c_sink, final_g):
    bn = x.shape[0]
    x = x.reshape(bn * SEQ, D_MODEL)
    for l, lw in enumerate(layers):
        a, b, c, d0, d1, d2 = _project(x, lw, tables)
        seq = lambda t: t.reshape(bn, -1, t.shape[-1])
        ys = (
            _latent_attention(seq(a)),
            _axial_attention(seq(b)),
            _window_attention(seq(c), c_sink[l]),
            _dilated_attention([seq(d0), seq(d1), seq(d2)]),
        )
        ys = [y.reshape(bn * SEQ, BRANCH_W) for y in ys]
        x = _merge(x, ys, lw, final_g, final=(l == DEPTH - 1))
    return x.reshape(bn, SEQ, D_MODEL)


def _prepare(norm_in, w_in, a_q_norm, w_q_up, a_kv_norm, w_kv_up, b_q_norm, b_k_norm, c_sink, w_branch, w_out, final_norm):
    layers = [_layer_weights(l, norm_in, w_in, a_q_norm, w_q_up, a_kv_norm, w_kv_up, b_q_norm, b_k_norm, w_branch, w_out)
              for l in range(DEPTH)]
    return layers, _rope_tables(), c_sink, final_norm.reshape(1, D_MODEL)


def _trunk(x, *params):
    return _trunk_prepared(x, *_prepare(*params))


def kernel(x_prompt, x_sample, norm_in, w_in, a_q_norm, w_q_up, a_kv_norm, w_kv_up, b_q_norm, b_k_norm, c_sink, w_branch, w_out, final_norm):
    prepared = _prepare(norm_in, w_in, a_q_norm, w_q_up, a_kv_norm, w_kv_up, b_q_norm, b_k_norm, c_sink, w_branch, w_out, final_norm)
    return (_trunk_prepared(x_prompt, *prepared), _trunk_prepared(x_sample, *prepared))
```

```python
import functools

import jax
import jax.numpy as jnp
from jax import lax
from jax.experimental import pallas as pl
from jax.experimental.pallas import tpu as pltpu

D_MODEL = 1024
SEQ = 2048
DEPTH = 2
GRID_W = 64
ROPE_THETA = 10000.0
EPS = 1e-6
NEG = -1e30
HEAD_DIM = 64
LANES = 128
BRANCH_W = 256
N_BRANCH = 4
A_HEADS = 4
A_NOPE = 64
A_ROPE = 32
A_V = 64
A_Q_LORA = 256
A_KV_LORA = 128
A_IN = A_Q_LORA + A_KV_LORA + A_ROPE
QKV_W = 512
C_WINDOW = 128
D_GROUPS = ((128, 1), (512, 4), (2048, 16))
D_BAND = 64
GATE_OFF = A_IN + 2 * QKV_W + len(D_GROUPS) * QKV_W
MERGE_OFF = GATE_OFF + N_BRANCH * BRANCH_W
LOG2E = 1.4426950408889634
A_SCALE = (A_NOPE + A_ROPE) ** -0.5 * LOG2E
QK_SCALE = HEAD_DIM ** -0.5 * LOG2E

TM = 1024
TQ = 256
DENSE_TQ = 1024
DENSE_KC = 1024
VMEM_LIMIT = 56 * 1024 * 1024

F32 = jnp.float32
BF16 = jnp.bfloat16


def _dot(a, b):
    return jnp.dot(a, b, preferred_element_type=F32)


def _dot_nt(a, b):
    return lax.dot_general(a, b, (((1,), (1,)), ((), ())), preferred_element_type=F32)


def _rms(x, g):
    return x * lax.rsqrt(jnp.mean(x * x, axis=-1, keepdims=True) + EPS) * g


def _sigmoid(x):
    return 1.0 / (1.0 + jnp.exp(-x))


def _rope(a, cos, sin, shift):
    lane = lax.broadcasted_iota(jnp.int32, (1, LANES), 1)
    first = (lane % (2 * shift)) < shift
    outs = []
    for j in range(a.shape[1] // LANES):
        s = a[:, j * LANES:(j + 1) * LANES]
        partner = jnp.where(first, pltpu.roll(s, LANES - shift, 1), pltpu.roll(s, shift, 1))
        outs.append(s * cos + partner * sin)
    return outs[0] if len(outs) == 1 else jnp.concatenate(outs, axis=1)


def _head_mean_sq(x, gm):
    x2 = x * x
    hi = x2.astype(BF16)
    lo = (x2 - hi.astype(F32)).astype(BF16)
    return _dot(hi, gm) + _dot(lo, gm)


def _proj_body(x_ref, g_ref, w_ref, wq_ref, wkv_ref, gq_ref, gkv_ref, gbq_ref, gbk_ref, gm_ref, tab_ref,
               a_ref, b_ref, c_ref, d0_ref, d1_ref, d2_ref, stage_ref):
    xn = _rms(x_ref[...], g_ref[...]).astype(BF16)
    cos_a, sin_a = tab_ref[0], tab_ref[1]
    cos_b, sin_b = tab_ref[2], tab_ref[3]
    cos_c, sin_c = tab_ref[4], tab_ref[5]

    h = _dot(xn, w_ref[:, 0:512])
    q = _dot(_rms(h[:, 0:256], gq_ref[...]).astype(BF16), wq_ref[...])
    q = _rope(q, cos_a, sin_a, A_ROPE // 2) * A_SCALE
    kv = _dot(_rms(h[:, 256:384], gkv_ref[...]).astype(BF16), wkv_ref[...])
    kr = _rope(h[:, 384:512], cos_a, sin_a, A_ROPE // 2)
    k = kv[:, 0:512] + jnp.concatenate([kr] * A_HEADS, axis=1)
    a_ref[:, 0:512] = q.astype(BF16)
    a_ref[:, 512:1024] = k.astype(BF16)
    a_ref[:, 1024:1280] = kv[:, 512:768].astype(BF16)

    h = _dot(xn, w_ref[:, 512:1024])
    q, k = h[:, 0:256], h[:, 256:384]
    q = q * lax.rsqrt(_head_mean_sq(q, gm_ref[...]) + EPS) * gbq_ref[...]
    k = k * lax.rsqrt(_head_mean_sq(k, gm_ref[0:128, 0:128]) + EPS) * gbk_ref[...]
    b_ref[:, 0:256] = (_rope(q, cos_b, sin_b, HEAD_DIM // 4) * QK_SCALE).astype(BF16)
    b_ref[:, 256:384] = _rope(k, cos_b, sin_b, HEAD_DIM // 4).astype(BF16)
    b_ref[:, 384:512] = h[:, 384:512].astype(BF16)

    for n, (o_ref, dil) in enumerate(((c_ref, 1), (d0_ref, 1), (d1_ref, 4), (d2_ref, 16))):
        h = _dot(xn, w_ref[:, 1024 + n * QKV_W:1024 + (n + 1) * QKV_W])
        q = _rope(h[:, 0:256], cos_c, sin_c, HEAD_DIM // 2) * QK_SCALE
        k = _rope(h[:, 256:384], cos_c, sin_c, HEAD_DIM // 2)
        if dil == 1:
            o_ref[:, 0:256] = q.astype(BF16)
            o_ref[:, 256:384] = k.astype(BF16)
            o_ref[:, 384:512] = h[:, 384:512].astype(BF16)
        else:
            for s, slab in enumerate((q[:, 0:128], q[:, 128:256], k, h[:, 384:512])):
                stage_ref[s] = slab
            for c in range(dil):
                for s in range(QKV_W // LANES):
                    lanes = slice(c * QKV_W + s * LANES, c * QKV_W + (s + 1) * LANES)
                    o_ref[:, lanes] = stage_ref[s, pl.ds(c, TM // dil, stride=dil), :].astype(BF16)


def _const_spec(shape):
    return pl.BlockSpec(shape, lambda i: (0,) * len(shape), pipeline_mode=pl.Buffered(1))


def _project(x, lw, tables):
    n = x.shape[0]
    nt = SEQ // TM
    row = lambda w, dil=1: pl.BlockSpec((TM // dil, dil * w), lambda i: (i, 0))
    outs = ((1280, 1), (QKV_W, 1), (QKV_W, 1)) + tuple((QKV_W, dil) for _, dil in D_GROUPS)
    return pl.pallas_call(
        _proj_body,
        grid=(n // TM,),
        in_specs=[
            row(D_MODEL),
            _const_spec((1, D_MODEL)),
            _const_spec((D_MODEL, 3072)),
            _const_spec((A_Q_LORA, 512)),
            _const_spec((A_KV_LORA, 768)),
            _const_spec((1, A_Q_LORA)),
            _const_spec((1, A_KV_LORA)),
            _const_spec((1, 256)),
            _const_spec((1, 128)),
            _const_spec((256, 256)),
            pl.BlockSpec((6, TM, LANES), lambda i: (0, i % nt, 0)),
        ],
        out_specs=[row(w, dil) for w, dil in outs],
        out_shape=[jax.ShapeDtypeStruct((n // dil, dil * w), BF16) for w, dil in outs],
        scratch_shapes=[pltpu.VMEM((QKV_W // LANES, TM, LANES), F32)],
        compiler_params=pltpu.CompilerParams(dimension_semantics=("parallel",), vmem_limit_bytes=VMEM_LIMIT),
        name="proj",
    )(x, lw["g_in"], lw["w_attn"], lw["wq"], lw["wkv"], lw["gq"], lw["gkv"], lw["gbq"], lw["gbk"], lw["gm"], tables)


def _lane_lo():
    return lax.broadcasted_iota(jnp.int32, (1, LANES), 1) < HEAD_DIM


def _attend(q, k_ref, k_lanes, v_ref):
    m = l = acc = None
    for c in range(SEQ // DENSE_KC):
        keys = slice(c * DENSE_KC, (c + 1) * DENSE_KC)
        s = _dot_nt(q, k_ref[0, keys, k_lanes])
        m_c = jnp.max(s, axis=-1, keepdims=True)
        if m is None:
            m = m_c
            p = jnp.exp2(s - m)
            l = jnp.sum(p, axis=-1, keepdims=True)
            acc = _dot(p.astype(BF16), v_ref[keys])
        else:
            m_new = jnp.maximum(m, m_c)
            scale = jnp.exp2(m - m_new)
            p = jnp.exp2(s - m_new)
            l = scale * l + jnp.sum(p, axis=-1, keepdims=True)
            acc = scale * acc + _dot(p.astype(BF16), v_ref[keys])
            m = m_new
    return acc * (1.0 / l)


def _dense_body(q_ref, k_ref, v_ref, o_ref, vm_ref, *, shared):
    lo = _lane_lo()
    zero = jnp.zeros((), BF16)
    v = v_ref[0]
    vm_ref[0] = jnp.where(lo, v, zero)
    vm_ref[1] = jnp.where(lo, zero, v)

    for r in range(0, SEQ, DENSE_TQ):
        rows = slice(r, r + DENSE_TQ)
        if shared:
            qs = q_ref[0, rows, :]
            q_a, q_b = jnp.where(lo, qs, zero), jnp.where(lo, zero, qs)
            lanes_a = lanes_b = slice(0, LANES)
        else:
            q_a, q_b = q_ref[0, rows, 0:LANES], q_ref[0, rows, LANES:2 * LANES]
            lanes_a, lanes_b = slice(0, LANES), slice(LANES, 2 * LANES)
        out = _attend(q_a, k_ref, lanes_a, vm_ref.at[0]) + _attend(q_b, k_ref, lanes_b, vm_ref.at[1])
        o_ref[0, rows, :] = out.astype(BF16)


def _attn_params(n_grid):
    return pltpu.CompilerParams(dimension_semantics=("parallel",) * n_grid, vmem_limit_bytes=VMEM_LIMIT)


def _latent_attention(a):
    bn = a.shape[0]
    return pl.pallas_call(
        functools.partial(_dense_body, shared=False),
        grid=(bn, 2),
        in_specs=[
            pl.BlockSpec((1, SEQ, 256), lambda b, s: (b, 0, s)),
            pl.BlockSpec((1, SEQ, 256), lambda b, s: (b, 0, 2 + s)),
            pl.BlockSpec((1, SEQ, LANES), lambda b, s: (b, 0, 8 + s)),
        ],
        out_specs=pl.BlockSpec((1, SEQ, LANES), lambda b, s: (b, 0, s)),
        out_shape=jax.ShapeDtypeStruct((bn, SEQ, BRANCH_W), BF16),
        scratch_shapes=[pltpu.VMEM((2, SEQ, LANES), BF16)],
        compiler_params=_attn_params(2),
        name="attn_latent",
    )(a, a, a)


def _axial_attention(qkv):
    bn = qkv.shape[0]
    return pl.pallas_call(
        functools.partial(_dense_body, shared=True),
        grid=(bn, 2),
        in_specs=[
            pl.BlockSpec((1, SEQ, LANES), lambda b, s: (b, 0, s)),
            pl.BlockSpec((1, SEQ, LANES), lambda b, s: (b, 0, 2)),
            pl.BlockSpec((1, SEQ, LANES), lambda b, s: (b, 0, 3)),
        ],
        out_specs=pl.BlockSpec((1, SEQ, LANES), lambda b, s: (b, 0, s)),
        out_shape=jax.ShapeDtypeStruct((bn, SEQ, BRANCH_W), BF16),
        scratch_shapes=[pltpu.VMEM((2, SEQ, LANES), BF16)],
        compiler_params=_attn_params(2),
        name="attn_axial",
    )(qkv, qkv, qkv)


BQ = 128
BLOCKS_PER_STEP = 8


def _band_window(r, length, tk, band):
    ws = min(max(r - band, 0), length - tk)
    return ws, (r - ws) // band


def _fill_band_bias(bias_ref, tk, band):
    kpos = lax.broadcasted_iota(jnp.int32, (tk, 1), 0)
    qpos = lax.broadcasted_iota(jnp.int32, (1, 2 * BQ), 1) & (BQ - 1)
    for kind in range(3):
        bias_ref[kind] = jnp.where(jnp.abs(qpos + kind * band - kpos) <= band, 0.0, NEG).astype(F32)


def _banded_blocks(blocks, sinks=None, want_lse=False):
    lo = _lane_lo()
    zero = jnp.zeros((), BF16)
    kv0_rows = lax.broadcasted_iota(jnp.int32, (LANES, 1), 0) < HEAD_DIM
    scores = []
    for q0, q1, k, _, bias in blocks:
        qq = jnp.concatenate([q0, q1], axis=0)
        scores.append(_dot_nt(jnp.where(lo, k, zero), qq) + bias)
        scores.append(_dot_nt(jnp.where(lo, zero, k), qq) + bias)
    s = jnp.stack(scores)
    m = jnp.max(s, axis=1, keepdims=True)
    p = jnp.exp2(s - m)
    l = jnp.sum(p, axis=1, keepdims=True)
    if sinks is not None:
        l = l + jnp.exp2(sinks - m)
    p = p.astype(BF16)
    inv = 1.0 / l
    lse = m + jnp.log2(l) if want_lse else None
    results = []
    for b, (_, _, _, v, _) in enumerate(blocks):
        o_lo = lax.dot_general(v, p[2 * b], (((0,), (0,)), ((), ())), preferred_element_type=F32) * inv[2 * b]
        o_hi = lax.dot_general(v, p[2 * b + 1], (((0,), (0,)), ((), ())), preferred_element_type=F32) * inv[2 * b + 1]
        out = jnp.where(kv0_rows, o_lo, o_hi).T
        results.append((out, jnp.where(kv0_rows, lse[2 * b], lse[2 * b + 1]).T if want_lse else None))
    return results


C_KEYS = BQ + 2 * C_WINDOW
D_KEYS = BQ + 2 * D_BAND


def _window_body(sink_ref, qkv_ref, o_ref, bias_ref):
    _fill_band_bias(bias_ref, C_KEYS, C_WINDOW)
    lane = lax.broadcasted_iota(jnp.int32, (1, 2 * BQ), 1)
    sink_kv = [jnp.where(lane < BQ, sink_ref[2 * h], sink_ref[2 * h + 1]) * LOG2E for h in range(2)]
    sinks = jnp.stack(sink_kv * BLOCKS_PER_STEP)

    for r0 in range(0, SEQ, BQ * BLOCKS_PER_STEP):
        blocks, rows = [], []
        for r in range(r0, r0 + BQ * BLOCKS_PER_STEP, BQ):
            ws, kind = _band_window(r, SEQ, C_KEYS, C_WINDOW)
            blocks.append((qkv_ref[0, r:r + BQ, 0:128], qkv_ref[0, r:r + BQ, 128:256],
                           qkv_ref[0, ws:ws + C_KEYS, 256:384], qkv_ref[0, ws:ws + C_KEYS, 384:512], bias_ref[kind]))
            rows.append(r)
        for r, (out, _) in zip(rows, _banded_blocks(blocks, sinks)):
            o_ref[0, r:r + BQ, 0:128] = out[0:BQ].astype(BF16)
            o_ref[0, r:r + BQ, 128:256] = out[BQ:2 * BQ].astype(BF16)


def _window_attention(qkv, sink):
    bn = qkv.shape[0]
    return pl.pallas_call(
        _window_body,
        grid=(bn,),
        in_specs=[
            pl.BlockSpec(memory_space=pltpu.MemorySpace.SMEM),
            pl.BlockSpec((1, SEQ, QKV_W), lambda b: (b, 0, 0)),
        ],
        out_specs=pl.BlockSpec((1, SEQ, BRANCH_W), lambda b: (b, 0, 0)),
        out_shape=jax.ShapeDtypeStruct((bn, SEQ, BRANCH_W), BF16),
        scratch_shapes=[pltpu.VMEM((3, C_KEYS, 2 * BQ), F32)],
        compiler_params=_attn_params(1),
        name="attn_window",
    )(sink, qkv)


def _dilated_body(d0_ref, d1_ref, d2_ref, o_ref, acc_ref, lse_ref, bias_ref):
    _fill_band_bias(bias_ref, D_KEYS, D_BAND)

    def block(ref, dil, c, r):
        length = SEQ // dil
        tk = min(D_KEYS, length)
        ws, kind = _band_window(r, length, tk, D_BAND)
        base = c * QKV_W
        return (ref[0, r:r + BQ, base:base + 128], ref[0, r:r + BQ, base + 128:base + 256],
                ref[0, ws:ws + tk, base + 256:base + 384], ref[0, ws:ws + tk, base + 384:base + 512],
                bias_ref[kind, 0:tk, :])

    def scatter(g, dil, c, r, out, lse):
        rows = pl.ds(r * dil + c, BQ, stride=dil) if dil > 1 else pl.ds(r, BQ)
        for slab in range(2):
            acc_ref[g, slab, rows, :] = out[slab * BQ:(slab + 1) * BQ]
            lse_ref[g, slab, rows, :] = lse[slab * BQ:(slab + 1) * BQ]

    for g, (ref, (_, dil)) in enumerate(zip((d0_ref, d1_ref, d2_ref), D_GROUPS)):
        todo = [(c, r) for c in range(dil) for r in range(0, SEQ // dil, BQ)]
        for i in range(0, len(todo), BLOCKS_PER_STEP):
            batch = todo[i:i + BLOCKS_PER_STEP]
            for (c, r), (out, lse) in zip(batch, _banded_blocks([block(ref, dil, c, r) for c, r in batch], want_lse=True)):
                scatter(g, dil, c, r, out, lse)

    def merge(i, carry):
        rows = pl.ds(pl.multiple_of(i * TQ, TQ), TQ)
        for slab in range(2):
            l0, l1, l2 = lse_ref[0, slab, rows, :], lse_ref[1, slab, rows, :], lse_ref[2, slab, rows, :]
            m = jnp.maximum(jnp.maximum(l0, l1), l2)
            w0, w1, w2 = jnp.exp2(l0 - m), jnp.exp2(l1 - m), jnp.exp2(l2 - m)
            num = w0 * acc_ref[0, slab, rows, :] + w1 * acc_ref[1, slab, rows, :] + w2 * acc_ref[2, slab, rows, :]
            o_ref[0, rows, slab * LANES:(slab + 1) * LANES] = (num * (1.0 / (w0 + w1 + w2))).astype(BF16)
        return carry

    lax.fori_loop(0, SEQ // TQ, merge, 0)


def _dilated_attention(views):
    bn = views[0].shape[0]
    return pl.pallas_call(
        _dilated_body,
        grid=(bn,),
        in_specs=[pl.BlockSpec((1,) + v.shape[1:], lambda b: (b, 0, 0)) for v in views],
        out_specs=pl.BlockSpec((1, SEQ, BRANCH_W), lambda b: (b, 0, 0)),
        out_shape=jax.ShapeDtypeStruct((bn, SEQ, BRANCH_W), BF16),
        scratch_shapes=[pltpu.VMEM((3, 2, SEQ, LANES), F32), pltpu.VMEM((3, 2, SEQ, LANES), F32),
                        pltpu.VMEM((3, D_KEYS, 2 * BQ), F32)],
        compiler_params=_attn_params(1),
        name="attn_dilated",
    )(*views)


def _merge_body(x_ref, g_ref, ya_ref, yb_ref, yc_ref, yd_ref, wz_ref, wmg_ref, wb_ref, wo_ref, fg_ref, o_ref, *, final):
    x = x_ref[...]
    xn = _rms(x, g_ref[...]).astype(BF16)
    merged = jnp.zeros((TM, D_MODEL), F32)
    for i, y_ref in enumerate((ya_ref, yb_ref, yc_ref, yd_ref)):
        z = _dot(xn, wz_ref[:, i * BRANCH_W:(i + 1) * BRANCH_W])
        y = y_ref[...].astype(F32) * (z * _sigmoid(z))
        branch = _dot(y.astype(BF16), wb_ref[i])
        gate = _sigmoid(_dot(xn, wmg_ref[:, i * D_MODEL:(i + 1) * D_MODEL]))
        merged = merged + gate * branch
    out = x + _dot(merged.astype(BF16), wo_ref[...])
    if final:
        out = _rms(out, fg_ref[...])
    o_ref[...] = out


def _merge(x, ys, lw, final_g, final):
    n = x.shape[0]
    row = lambda w: pl.BlockSpec((TM, w), lambda i: (i, 0))
    return pl.pallas_call(
        functools.partial(_merge_body, final=final),
        grid=(n // TM,),
        in_specs=[
            row(D_MODEL),
            _const_spec((1, D_MODEL)),
            row(BRANCH_W), row(BRANCH_W), row(BRANCH_W), row(BRANCH_W),
            _const_spec((D_MODEL, N_BRANCH * BRANCH_W)),
            _const_spec((D_MODEL, N_BRANCH * D_MODEL)),
            _const_spec((N_BRANCH, BRANCH_W, D_MODEL)),
            _const_spec((D_MODEL, D_MODEL)),
            _const_spec((1, D_MODEL)),
        ],
        out_specs=row(D_MODEL),
        out_shape=jax.ShapeDtypeStruct((n, D_MODEL), F32),
        compiler_params=pltpu.CompilerParams(dimension_semantics=("parallel",), vmem_limit_bytes=VMEM_LIMIT),
        name="merge",
    )(x, lw["g_in"], *ys, lw["wz"], lw["wmg"], lw["wb"], lw["wo"], final_g)


def _gqa_cols(w):
    q = w[..., 0:256].reshape(w.shape[:-1] + (4, HEAD_DIM))[..., (0, 2, 1, 3), :].reshape(w.shape[:-1] + (256,))
    return jnp.concatenate([q, w[..., 256:512]], axis=-1)


def _slab_heads(w, axis):
    w = jnp.moveaxis(w, axis, -1)
    w = w.reshape(w.shape[:-1] + (4, HEAD_DIM))[..., (0, 2, 1, 3), :].reshape(w.shape[:-1] + (256,))
    return jnp.moveaxis(w, -1, axis)


def _layer_weights(l, norm_in, w_in, a_q_norm, w_q_up, a_kv_norm, w_kv_up, b_q_norm, b_k_norm, w_branch, w_out):
    w = w_in[l]
    zeros = lambda c: jnp.zeros((D_MODEL, c), F32)
    off_b = A_IN
    cols = [w[:, 0:384], zeros(64), w[:, 384:416], zeros(32)]
    for n in range(2 + len(D_GROUPS)):
        cols.append(_gqa_cols(w[:, off_b + n * QKV_W:off_b + (n + 1) * QKV_W]))
    w_attn = jnp.concatenate(cols, axis=1)

    wq = w_q_up[l].reshape(A_Q_LORA, A_HEADS, A_NOPE + A_ROPE)
    wq = jnp.pad(wq, ((0, 0), (0, 0), (0, LANES - A_NOPE - A_ROPE))).reshape(A_Q_LORA, A_HEADS * LANES)
    wkv = w_kv_up[l].reshape(A_KV_LORA, A_HEADS, A_NOPE + A_V)
    wk = jnp.pad(wkv[:, :, :A_NOPE], ((0, 0), (0, 0), (0, LANES - A_NOPE))).reshape(A_KV_LORA, A_HEADS * LANES)
    wv = wkv[:, :, A_NOPE:].reshape(A_KV_LORA, A_HEADS * A_V)

    wz = w[:, GATE_OFF:MERGE_OFF]
    wz = jnp.concatenate([wz[:, 0:256]] + [_slab_heads(wz[:, i * 256:(i + 1) * 256], 1) for i in (1, 2, 3)], axis=1)
    wb = jnp.stack([w_branch[l, 0]] + [_slab_heads(w_branch[l, i], 0) for i in (1, 2, 3)])

    head_id = jnp.arange(256) // HEAD_DIM
    return {
        "g_in": norm_in[l].reshape(1, D_MODEL),
        "w_attn": w_attn.astype(BF16),
        "wq": wq.astype(BF16),
        "wkv": jnp.concatenate([wk, wv], axis=1).astype(BF16),
        "gq": a_q_norm[l].reshape(1, A_Q_LORA),
        "gkv": a_kv_norm[l].reshape(1, A_KV_LORA),
        "gbq": jnp.tile(b_q_norm[l], 4).reshape(1, 256),
        "gbk": jnp.tile(b_k_norm[l], 2).reshape(1, 128),
        "gm": jnp.where(head_id[:, None] == head_id[None, :], 1.0 / HEAD_DIM, 0.0).astype(BF16),
        "wz": wz.astype(BF16),
        "wmg": w[:, MERGE_OFF:].astype(BF16),
        "wb": wb.astype(BF16),
        "wo": w_out[l].astype(BF16),
    }


def _rope_tables():
    pos = jnp.arange(SEQ)
    rows = jnp.repeat(jnp.arange(SEQ // GRID_W), GRID_W)
    cols = jnp.tile(jnp.arange(GRID_W), SEQ // GRID_W)

    def ang(p, d):
        freqs = ROPE_THETA ** (-jnp.arange(d // 2, dtype=F32) * 2.0 / d)
        return p.astype(F32)[:, None] * freqs[None, :]

    def pair(a):
        return jnp.concatenate([jnp.cos(a), jnp.cos(a)], axis=1), jnp.concatenate([-jnp.sin(a), jnp.sin(a)], axis=1)

    one, zero = jnp.ones((SEQ, 1), F32), jnp.zeros((SEQ, 1), F32)
    ca, sa = pair(ang(pos, A_ROPE))
    cos_a = jnp.concatenate([jnp.tile(one, (1, 64)), ca, jnp.tile(one, (1, 32))], axis=1)
    sin_a = jnp.concatenate([jnp.tile(zero, (1, 64)), sa, jnp.tile(zero, (1, 32))], axis=1)
    (cr, sr), (cc, sc) = pair(ang(rows, HEAD_DIM // 2)), pair(ang(cols, HEAD_DIM // 2))
    cos_b, sin_b = jnp.tile(jnp.concatenate([cr, cc], axis=1), (1, 2)), jnp.tile(jnp.concatenate([sr, sc], axis=1), (1, 2))
    cp, sp = pair(ang(pos, HEAD_DIM))
    return jnp.stack([cos_a, sin_a, cos_b, sin_b, jnp.tile(cp, (1, 2)), jnp.tile(sp, (1, 2))])


def _trunk_prepared(x, layers, tables, c_sink, final_g):
    bn = x.shape[0]
    x = x.reshape(bn * SEQ, D_MODEL)
    for l, lw in enumerate(layers):
        a, b, c, d0, d1, d2 = _project(x, lw, tables)
        seq = lambda t: t.reshape(bn, -1, t.shape[-1])
        ys = (
            _latent_attention(seq(a)),
            _axial_attention(seq(b)),
            _window_attention(seq(c), c_sink[l]),
            _dilated_attention([seq(d0), seq(d1), seq(d2)]),
        )
        ys = [y.reshape(bn * SEQ, BRANCH_W) for y in ys]
        x = _merge(x, ys, lw, final_g, final=(l == DEPTH - 1))
    return x.reshape(bn, SEQ, D_MODEL)


def _prepare(norm_in, w_in, a_q_norm, w_q_up, a_kv_norm, w_kv_up, b_q_norm, b_k_norm, c_sink, w_branch, w_out, final_norm):
    layers = [_layer_weights(l, norm_in, w_in, a_q_norm, w_q_up, a_kv_norm, w_kv_up, b_q_norm, b_k_norm, w_branch, w_out)
              for l in range(DEPTH)]
    return layers, _rope_tables(), c_sink, final_norm.reshape(1, D_MODEL)


def _trunk(x, *params):
    return _trunk_prepared(x, *_prepare(*params))


def kernel(x_prompt, x_sample, norm_in, w_in, a_q_norm, w_q_up, a_kv_norm, w_kv_up, b_q_norm, b_k_norm, c_sink, w_branch, w_out, final_norm):
    prepared = _prepare(norm_in, w_in, a_q_norm, w_q_up, a_kv_norm, w_kv_up, b_q_norm, b_k_norm, c_sink, w_branch, w_out, final_norm)
    return (_trunk_prepared(x_prompt, *prepared), _trunk_prepared(x_sample, *prepared))
```

```python
import functools

import jax
import jax.numpy as jnp
from jax import lax
from jax.experimental import pallas as pl
from jax.experimental.pallas import tpu as pltpu

D_MODEL = 1024
SEQ = 2048
DEPTH = 2
GRID_W = 64
ROPE_THETA = 10000.0
EPS = 1e-6
NEG = -1e30
HEAD_DIM = 64
LANES = 128
BRANCH_W = 256
N_BRANCH = 4
A_HEADS = 4
A_NOPE = 64
A_ROPE = 32
A_V = 64
A_Q_LORA = 256
A_KV_LORA = 128
A_IN = A_Q_LORA + A_KV_LORA + A_ROPE
QKV_W = 512
C_WINDOW = 128
D_GROUPS = ((128, 1), (512, 4), (2048, 16))
D_BAND = 64
GATE_OFF = A_IN + 2 * QKV_W + len(D_GROUPS) * QKV_W
MERGE_OFF = GATE_OFF + N_BRANCH * BRANCH_W
LOG2E = 1.4426950408889634
A_SCALE = (A_NOPE + A_ROPE) ** -0.5 * LOG2E
QK_SCALE = HEAD_DIM ** -0.5 * LOG2E

TM = 1024
TQ = 256
DENSE_TQ = 1024
DENSE_KC = 1024
VMEM_LIMIT = 56 * 1024 * 1024

F32 = jnp.float32
BF16 = jnp.bfloat16


def _dot(a, b):
    return jnp.dot(a, b, preferred_element_type=F32)


def _dot_nt(a, b):
    return lax.dot_general(a, b, (((1,), (1,)), ((), ())), preferred_element_type=F32)


def _rms(x, g):
    return x * lax.rsqrt(jnp.mean(x * x, axis=-1, keepdims=True) + EPS) * g


def _sigmoid(x):
    return 1.0 / (1.0 + jnp.exp(-x))


def _rope(a, cos, sin, shift):
    lane = lax.broadcasted_iota(jnp.int32, (1, LANES), 1)
    first = (lane % (2 * shift)) < shift
    outs = []
    for j in range(a.shape[1] // LANES):
        s = a[:, j * LANES:(j + 1) * LANES]
        partner = jnp.where(first, pltpu.roll(s, LANES - shift, 1), pltpu.roll(s, shift, 1))
        outs.append(s * cos + partner * sin)
    return outs[0] if len(outs) == 1 else jnp.concatenate(outs, axis=1)


def _head_mean_sq(x, gm):
    x2 = x * x
    hi = x2.astype(BF16)
    lo = (x2 - hi.astype(F32)).astype(BF16)
    return _dot(hi, gm) + _dot(lo, gm)


def _proj_body(x_ref, g_ref, w_ref, wq_ref, wkv_ref, gq_ref, gkv_ref, gbq_ref, gbk_ref, gm_ref, tab_ref,
               a_ref, b_ref, c_ref, d0_ref, d1_ref, d2_ref, stage_ref):
    xn = _rms(x_ref[...], g_ref[...]).astype(BF16)
    cos_a, sin_a = tab_ref[0], tab_ref[1]
    cos_b, sin_b = tab_ref[2], tab_ref[3]
    cos_c, sin_c = tab_ref[4], tab_ref[5]

    h = _dot(xn, w_ref[:, 0:512])
    q = _dot(_rms(h[:, 0:256], gq_ref[...]).astype(BF16), wq_ref[...])
    q = _rope(q, cos_a, sin_a, A_ROPE // 2) * A_SCALE
    kv = _dot(_rms(h[:, 256:384], gkv_ref[...]).astype(BF16), wkv_ref[...])
    kr = _rope(h[:, 384:512], cos_a, sin_a, A_ROPE // 2)
    k = kv[:, 0:512] + jnp.concatenate([kr] * A_HEADS, axis=1)
    a_ref[:, 0:512] = q.astype(BF16)
    a_ref[:, 512:1024] = k.astype(BF16)
    a_ref[:, 1024:1280] = kv[:, 512:768].astype(BF16)

    h = _dot(xn, w_ref[:, 512:1024])
    q, k = h[:, 0:256], h[:, 256:384]
    q = q * lax.rsqrt(_head_mean_sq(q, gm_ref[...]) + EPS) * gbq_ref[...]
    k = k * lax.rsqrt(_head_mean_sq(k, gm_ref[0:128, 0:128]) + EPS) * gbk_ref[...]
    b_ref[:, 0:256] = (_rope(q, cos_b, sin_b, HEAD_DIM // 4) * QK_SCALE).astype(BF16)
    b_ref[:, 256:384] = _rope(k, cos_b, sin_b, HEAD_DIM // 4).astype(BF16)
    b_ref[:, 384:512] = h[:, 384:512].astype(BF16)

    for n, (o_ref, dil) in enumerate(((c_ref, 1), (d0_ref, 1), (d1_ref, 4), (d2_ref, 16))):
        h = _dot(xn, w_ref[:, 1024 + n * QKV_W:1024 + (n + 1) * QKV_W])
        q = _rope(h[:, 0:256], cos_c, sin_c, HEAD_DIM // 2) * QK_SCALE
        k = _rope(h[:, 256:384], cos_c, sin_c, HEAD_DIM // 2)
        if dil == 1:
            o_ref[:, 0:256] = q.astype(BF16)
            o_ref[:, 256:384] = k.astype(BF16)
            o_ref[:, 384:512] = h[:, 384:512].astype(BF16)
        else:
            for s, slab in enumerate((q[:, 0:128], q[:, 128:256], k, h[:, 384:512])):
                stage_ref[s] = slab
            for c in range(dil):
                for s in range(QKV_W // LANES):
                    lanes = slice(c * QKV_W + s * LANES, c * QKV_W + (s + 1) * LANES)
                    o_ref[:, lanes] = stage_ref[s, pl.ds(c, TM // dil, stride=dil), :].astype(BF16)


def _const_spec(shape):
    return pl.BlockSpec(shape, lambda i: (0,) * len(shape), pipeline_mode=pl.Buffered(1))


def _project(x, lw, tables):
    n = x.shape[0]
    nt = SEQ // TM
    row = lambda w, dil=1: pl.BlockSpec((TM // dil, dil * w), lambda i: (i, 0))
    outs = ((1280, 1), (QKV_W, 1), (QKV_W, 1)) + tuple((QKV_W, dil) for _, dil in D_GROUPS)
    return pl.pallas_call(
        _proj_body,
        grid=(n // TM,),
        in_specs=[
            row(D_MODEL),
            _const_spec((1, D_MODEL)),
            _const_spec((D_MODEL, 3072)),
            _const_spec((A_Q_LORA, 512)),
            _const_spec((A_KV_LORA, 768)),
            _const_spec((1, A_Q_LORA)),
            _const_spec((1, A_KV_LORA)),
            _const_spec((1, 256)),
            _const_spec((1, 128)),
            _const_spec((256, 256)),
            pl.BlockSpec((6, TM, LANES), lambda i: (0, i % nt, 0)),
        ],
        out_specs=[row(w, dil) for w, dil in outs],
        out_shape=[jax.ShapeDtypeStruct((n // dil, dil * w), BF16) for w, dil in outs],
        scratch_shapes=[pltpu.VMEM((QKV_W // LANES, TM, LANES), F32)],
        compiler_params=pltpu.CompilerParams(dimension_semantics=("parallel",), vmem_limit_bytes=VMEM_LIMIT),
        name="proj",
    )(x, lw["g_in"], lw["w_attn"], lw["wq"], lw["wkv"], lw["gq"], lw["gkv"], lw["gbq"], lw["gbk"], lw["gm"], tables)


def _lane_lo():
    return lax.broadcasted_iota(jnp.int32, (1, LANES), 1) < HEAD_DIM


def _attend(q, k_ref, k_lanes, v_ref, sum_lane):
    m = acc = None
    for c in range(SEQ // DENSE_KC):
        keys = slice(c * DENSE_KC, (c + 1) * DENSE_KC)
        s = _dot_nt(q, k_ref[0, keys, k_lanes])
        m_c = jnp.max(s, axis=-1, keepdims=True)
        if m is None:
            m = m_c
            acc = _dot(jnp.exp2(s - m).astype(BF16), v_ref[keys])
        else:
            m_new = jnp.maximum(m, m_c)
            acc = jnp.exp2(m - m_new) * acc + _dot(jnp.exp2(s - m_new).astype(BF16), v_ref[keys])
            m = m_new
    lane = lax.broadcasted_iota(jnp.int32, (1, LANES), 1)
    return acc * (1.0 / jnp.sum(jnp.where(lane == sum_lane, acc, 0.0), axis=-1, keepdims=True))


def _dense_body(q_ref, k_ref, v_ref, o_ref, vm_ref, *, shared):
    lo = _lane_lo()
    zero = jnp.zeros((), BF16)
    v = v_ref[0].astype(F32)
    lane = lax.broadcasted_iota(jnp.int32, v.shape, 1)
    vm_ref[0] = jnp.where(lane < HEAD_DIM, v, jnp.where(lane == HEAD_DIM, 1.0, 0.0)).astype(BF16)
    vm_ref[1] = jnp.where(lane >= HEAD_DIM, v, jnp.where(lane == 0, 1.0, 0.0)).astype(BF16)

    for r in range(0, SEQ, DENSE_TQ):
        rows = slice(r, r + DENSE_TQ)
        if shared:
            qs = q_ref[0, rows, :]
            q_a, q_b = jnp.where(lo, qs, zero), jnp.where(lo, zero, qs)
            lanes_a = lanes_b = slice(0, LANES)
        else:
            q_a, q_b = q_ref[0, rows, 0:LANES], q_ref[0, rows, LANES:2 * LANES]
            lanes_a, lanes_b = slice(0, LANES), slice(LANES, 2 * LANES)
        out = jnp.where(lo, _attend(q_a, k_ref, lanes_a, vm_ref.at[0], HEAD_DIM), _attend(q_b, k_ref, lanes_b, vm_ref.at[1], 0))
        o_ref[0, rows, :] = out.astype(BF16)


def _attn_params(n_grid):
    return pltpu.CompilerParams(dimension_semantics=("parallel",) * n_grid, vmem_limit_bytes=VMEM_LIMIT)


def _latent_attention(a):
    bn = a.shape[0]
    return pl.pallas_call(
        functools.partial(_dense_body, shared=False),
        grid=(bn, 2),
        in_specs=[
            pl.BlockSpec((1, SEQ, 256), lambda b, s: (b, 0, s)),
            pl.BlockSpec((1, SEQ, 256), lambda b, s: (b, 0, 2 + s)),
            pl.BlockSpec((1, SEQ, LANES), lambda b, s: (b, 0, 8 + s)),
        ],
        out_specs=pl.BlockSpec((1, SEQ, LANES), lambda b, s: (b, 0, s)),
        out_shape=jax.ShapeDtypeStruct((bn, SEQ, BRANCH_W), BF16),
        scratch_shapes=[pltpu.VMEM((2, SEQ, LANES), BF16)],
        compiler_params=_attn_params(2),
        name="attn_latent",
    )(a, a, a)


def _axial_attention(qkv):
    bn = qkv.shape[0]
    return pl.pallas_call(
        functools.partial(_dense_body, shared=True),
        grid=(bn, 2),
        in_specs=[
            pl.BlockSpec((1, SEQ, LANES), lambda b, s: (b, 0, s)),
            pl.BlockSpec((1, SEQ, LANES), lambda b, s: (b, 0, 2)),
            pl.BlockSpec((1, SEQ, LANES), lambda b, s: (b, 0, 3)),
        ],
        out_specs=pl.BlockSpec((1, SEQ, LANES), lambda b, s: (b, 0, s)),
        out_shape=jax.ShapeDtypeStruct((bn, SEQ, BRANCH_W), BF16),
        scratch_shapes=[pltpu.VMEM((2, SEQ, LANES), BF16)],
        compiler_params=_attn_params(2),
        name="attn_axial",
    )(qkv, qkv, qkv)


BQ = 128
BLOCKS_PER_STEP = 8


def _band_window(r, length, tk, band):
    ws = min(max(r - band, 0), length - tk)
    return ws, (r - ws) // band


def _fill_band_bias(bias_ref, tk, band):
    kpos = lax.broadcasted_iota(jnp.int32, (tk, 1), 0)
    qpos = lax.broadcasted_iota(jnp.int32, (1, 2 * BQ), 1) & (BQ - 1)
    for kind in range(3):
        bias_ref[kind] = jnp.where(jnp.abs(qpos + kind * band - kpos) <= band, 0.0, NEG).astype(F32)


def _banded_blocks(blocks, sinks=None, want_lse=False):
    lo = _lane_lo()
    zero = jnp.zeros((), BF16)
    kv0_rows = lax.broadcasted_iota(jnp.int32, (LANES, 1), 0) < HEAD_DIM
    scores = []
    for q0, q1, k, _, bias in blocks:
        qq = jnp.concatenate([q0, q1], axis=0)
        scores.append(_dot_nt(jnp.where(lo, k, zero), qq) + bias)
        scores.append(_dot_nt(jnp.where(lo, zero, k), qq) + bias)
    s = jnp.stack(scores)
    m = jnp.max(s, axis=1, keepdims=True)
    p = jnp.exp2(s - m)
    l = jnp.sum(p, axis=1, keepdims=True)
    if sinks is not None:
        l = l + jnp.exp2(sinks - m)
    p = p.astype(BF16)
    inv = 1.0 / l
    lse = m + jnp.log2(l) if want_lse else None
    results = []
    for b, (_, _, _, v, _) in enumerate(blocks):
        o_lo = lax.dot_general(v, p[2 * b], (((0,), (0,)), ((), ())), preferred_element_type=F32) * inv[2 * b]
        o_hi = lax.dot_general(v, p[2 * b + 1], (((0,), (0,)), ((), ())), preferred_element_type=F32) * inv[2 * b + 1]
        out = jnp.where(kv0_rows, o_lo, o_hi).T
        results.append((out, jnp.where(kv0_rows, lse[2 * b], lse[2 * b + 1]).T if want_lse else None))
    return results


C_KEYS = BQ + 2 * C_WINDOW
D_KEYS = BQ + 2 * D_BAND


def _window_body(sink_ref, qkv_ref, o_ref, bias_ref):
    _fill_band_bias(bias_ref, C_KEYS, C_WINDOW)
    lane = lax.broadcasted_iota(jnp.int32, (1, 2 * BQ), 1)
    sink_kv = [jnp.where(lane < BQ, sink_ref[2 * h], sink_ref[2 * h + 1]) * LOG2E for h in range(2)]
    sinks = jnp.stack(sink_kv * BLOCKS_PER_STEP)

    for r0 in range(0, SEQ, BQ * BLOCKS_PER_STEP):
        blocks, rows = [], []
        for r in range(r0, r0 + BQ * BLOCKS_PER_STEP, BQ):
            ws, kind = _band_window(r, SEQ, C_KEYS, C_WINDOW)
            blocks.append((qkv_ref[0, r:r + BQ, 0:128], qkv_ref[0, r:r + BQ, 128:256],
                           qkv_ref[0, ws:ws + C_KEYS, 256:384], qkv_ref[0, ws:ws + C_KEYS, 384:512], bias_ref[kind]))
            rows.append(r)
        for r, (out, _) in zip(rows, _banded_blocks(blocks, sinks)):
            o_ref[0, r:r + BQ, 0:128] = out[0:BQ].astype(BF16)
            o_ref[0, r:r + BQ, 128:256] = out[BQ:2 * BQ].astype(BF16)


def _window_attention(qkv, sink):
    bn = qkv.shape[0]
    return pl.pallas_call(
        _window_body,
        grid=(bn,),
        in_specs=[
            pl.BlockSpec(memory_space=pltpu.MemorySpace.SMEM),
            pl.BlockSpec((1, SEQ, QKV_W), lambda b: (b, 0, 0)),
        ],
        out_specs=pl.BlockSpec((1, SEQ, BRANCH_W), lambda b: (b, 0, 0)),
        out_shape=jax.ShapeDtypeStruct((bn, SEQ, BRANCH_W), BF16),
        scratch_shapes=[pltpu.VMEM((3, C_KEYS, 2 * BQ), F32)],
        compiler_params=_attn_params(1),
        name="attn_window",
    )(sink, qkv)


def _dilated_body(d0_ref, d1_ref, d2_ref, o_ref, acc_ref, lse_ref, bias_ref):
    _fill_band_bias(bias_ref, D_KEYS, D_BAND)

    def block(ref, dil, c, r):
        length = SEQ // dil
        tk = min(D_KEYS, length)
        ws, kind = _band_window(r, length, tk, D_BAND)
        base = c * QKV_W
        return (ref[0, r:r + BQ, base:base + 128], ref[0, r:r + BQ, base + 128:base + 256],
                ref[0, ws:ws + tk, base + 256:base + 384], ref[0, ws:ws + tk, base + 384:base + 512],
                bias_ref[kind, 0:tk, :])

    def scatter(g, dil, c, r, out, lse):
        rows = pl.ds(r * dil + c, BQ, stride=dil) if dil > 1 else pl.ds(r, BQ)
        for slab in range(2):
            acc_ref[g, slab, rows, :] = out[slab * BQ:(slab + 1) * BQ]
            lse_ref[g, slab, rows, :] = lse[slab * BQ:(slab + 1) * BQ]

    for g, (ref, (_, dil)) in enumerate(zip((d0_ref, d1_ref, d2_ref), D_GROUPS)):
        todo = [(c, r) for c in range(dil) for r in range(0, SEQ // dil, BQ)]
        for i in range(0, len(todo), BLOCKS_PER_STEP):
            batch = todo[i:i + BLOCKS_PER_STEP]
            for (c, r), (out, lse) in zip(batch, _banded_blocks([block(ref, dil, c, r) for c, r in batch], want_lse=True)):
                scatter(g, dil, c, r, out, lse)

    def merge(i, carry):
        rows = pl.ds(pl.multiple_of(i * TQ, TQ), TQ)
        for slab in range(2):
            l0, l1, l2 = lse_ref[0, slab, rows, :], lse_ref[1, slab, rows, :], lse_ref[2, slab, rows, :]
            m = jnp.maximum(jnp.maximum(l0, l1), l2)
            w0, w1, w2 = jnp.exp2(l0 - m), jnp.exp2(l1 - m), jnp.exp2(l2 - m)
            num = w0 * acc_ref[0, slab, rows, :] + w1 * acc_ref[1, slab, rows, :] + w2 * acc_ref[2, slab, rows, :]
            o_ref[0, rows, slab * LANES:(slab + 1) * LANES] = (num * (1.0 / (w0 + w1 + w2))).astype(BF16)
        return carry

    lax.fori_loop(0, SEQ // TQ, merge, 0)


def _dilated_attention(views):
    bn = views[0].shape[0]
    return pl.pallas_call(
        _dilated_body,
        grid=(bn,),
        in_specs=[pl.BlockSpec((1,) + v.shape[1:], lambda b: (b, 0, 0)) for v in views],
        out_specs=pl.BlockSpec((1, SEQ, BRANCH_W), lambda b: (b, 0, 0)),
        out_shape=jax.ShapeDtypeStruct((bn, SEQ, BRANCH_W), BF16),
        scratch_shapes=[pltpu.VMEM((3, 2, SEQ, LANES), F32), pltpu.VMEM((3, 2, SEQ, LANES), F32),
                        pltpu.VMEM((3, D_KEYS, 2 * BQ), F32)],
        compiler_params=_attn_params(1),
        name="attn_dilated",
    )(*views)


def _merge_body(x_ref, g_ref, ya_ref, yb_ref, yc_ref, yd_ref, wz_ref, wmg_ref, wb_ref, wo_ref, fg_ref, o_ref, *, final):
    x = x_ref[...]
    xn = _rms(x, g_ref[...]).astype(BF16)
    merged = jnp.zeros((TM, D_MODEL), F32)
    for i, y_ref in enumerate((ya_ref, yb_ref, yc_ref, yd_ref)):
        z = _dot(xn, wz_ref[:, i * BRANCH_W:(i + 1) * BRANCH_W])
        y = y_ref[...].astype(F32) * (z * _sigmoid(z))
        branch = _dot(y.astype(BF16), wb_ref[i])
        gate = _sigmoid(_dot(xn, wmg_ref[:, i * D_MODEL:(i + 1) * D_MODEL]))
        merged = merged + gate * branch
    out = x + _dot(merged.astype(BF16), wo_ref[...])
    if final:
        out = _rms(out, fg_ref[...])
    o_ref[...] = out


def _merge(x, ys, lw, final_g, final):
    n = x.shape[0]
    row = lambda w: pl.BlockSpec((TM, w), lambda i: (i, 0))
    return pl.pallas_call(
        functools.partial(_merge_body, final=final),
        grid=(n // TM,),
        in_specs=[
            row(D_MODEL),
            _const_spec((1, D_MODEL)),
            row(BRANCH_W), row(BRANCH_W), row(BRANCH_W), row(BRANCH_W),
            _const_spec((D_MODEL, N_BRANCH * BRANCH_W)),
            _const_spec((D_MODEL, N_BRANCH * D_MODEL)),
            _const_spec((N_BRANCH, BRANCH_W, D_MODEL)),
            _const_spec((D_MODEL, D_MODEL)),
            _const_spec((1, D_MODEL)),
        ],
        out_specs=row(D_MODEL),
        out_shape=jax.ShapeDtypeStruct((n, D_MODEL), F32),
        compiler_params=pltpu.CompilerParams(dimension_semantics=("parallel",), vmem_limit_bytes=VMEM_LIMIT),
        name="merge",
    )(x, lw["g_in"], *ys, lw["wz"], lw["wmg"], lw["wb"], lw["wo"], final_g)


def _gqa_cols(w):
    q = w[..., 0:256].reshape(w.shape[:-1] + (4, HEAD_DIM))[..., (0, 2, 1, 3), :].reshape(w.shape[:-1] + (256,))
    return jnp.concatenate([q, w[..., 256:512]], axis=-1)


def _slab_heads(w, axis):
    w = jnp.moveaxis(w, axis, -1)
    w = w.reshape(w.shape[:-1] + (4, HEAD_DIM))[..., (0, 2, 1, 3), :].reshape(w.shape[:-1] + (256,))
    return jnp.moveaxis(w, -1, axis)


def _layer_weights(l, norm_in, w_in, a_q_norm, w_q_up, a_kv_norm, w_kv_up, b_q_norm, b_k_norm, w_branch, w_out):
    w = w_in[l]
    zeros = lambda c: jnp.zeros((D_MODEL, c), F32)
    off_b = A_IN
    cols = [w[:, 0:384], zeros(64), w[:, 384:416], zeros(32)]
    for n in range(2 + len(D_GROUPS)):
        cols.append(_gqa_cols(w[:, off_b + n * QKV_W:off_b + (n + 1) * QKV_W]))
    w_attn = jnp.concatenate(cols, axis=1)

    wq = w_q_up[l].reshape(A_Q_LORA, A_HEADS, A_NOPE + A_ROPE)
    wq = jnp.pad(wq, ((0, 0), (0, 0), (0, LANES - A_NOPE - A_ROPE))).reshape(A_Q_LORA, A_HEADS * LANES)
    wkv = w_kv_up[l].reshape(A_KV_LORA, A_HEADS, A_NOPE + A_V)
    wk = jnp.pad(wkv[:, :, :A_NOPE], ((0, 0), (0, 0), (0, LANES - A_NOPE))).reshape(A_KV_LORA, A_HEADS * LANES)
    wv = wkv[:, :, A_NOPE:].reshape(A_KV_LORA, A_HEADS * A_V)

    wz = w[:, GATE_OFF:MERGE_OFF]
    wz = jnp.concatenate([wz[:, 0:256]] + [_slab_heads(wz[:, i * 256:(i + 1) * 256], 1) for i in (1, 2, 3)], axis=1)
    wb = jnp.stack([w_branch[l, 0]] + [_slab_heads(w_branch[l, i], 0) for i in (1, 2, 3)])

    head_id = jnp.arange(256) // HEAD_DIM
    return {
        "g_in": norm_in[l].reshape(1, D_MODEL),
        "w_attn": w_attn.astype(BF16),
        "wq": wq.astype(BF16),
        "wkv": jnp.concatenate([wk, wv], axis=1).astype(BF16),
        "gq": a_q_norm[l].reshape(1, A_Q_LORA),
        "gkv": a_kv_norm[l].reshape(1, A_KV_LORA),
        "gbq": jnp.tile(b_q_norm[l], 4).reshape(1, 256),
        "gbk": jnp.tile(b_k_norm[l], 2).reshape(1, 128),
        "gm": jnp.where(head_id[:, None] == head_id[None, :], 1.0 / HEAD_DIM, 0.0).astype(BF16),
        "wz": wz.astype(BF16),
        "wmg": w[:, MERGE_OFF:].astype(BF16),
        "wb": wb.astype(BF16),
        "wo": w_out[l].astype(BF16),
    }


def _rope_tables():
    pos = jnp.arange(SEQ)
    rows = jnp.repeat(jnp.arange(SEQ // GRID_W), GRID_W)
    cols = jnp.tile(jnp.arange(GRID_W), SEQ // GRID_W)

    def ang(p, d):
        freqs = ROPE_THETA ** (-jnp.arange(d // 2, dtype=F32) * 2.0 / d)
        return p.astype(F32)[:, None] * freqs[None, :]

    def pair(a):
        return jnp.concatenate([jnp.cos(a), jnp.cos(a)], axis=1), jnp.concatenate([-jnp.sin(a), jnp.sin(a)], axis=1)

    one, zero = jnp.ones((SEQ, 1), F32), jnp.zeros((SEQ, 1), F32)
    ca, sa = pair(ang(pos, A_ROPE))
    cos_a = jnp.concatenate([jnp.tile(one, (1, 64)), ca, jnp.tile(one, (1, 32))], axis=1)
    sin_a = jnp.concatenate([jnp.tile(zero, (1, 64)), sa, jnp.tile(zero, (1, 32))], axis=1)
    (cr, sr), (cc, sc) = pair(ang(rows, HEAD_DIM // 2)), pair(ang(cols, HEAD_DIM // 2))
    cos_b, sin_b = jnp.tile(jnp.concatenate([cr, cc], axis=1), (1, 2)), jnp.tile(jnp.concatenate([sr, sc], axis=1), (1, 2))
    cp, sp = pair(ang(pos, HEAD_DIM))
    return jnp.stack([cos_a, sin_a, cos_b, sin_b, jnp.tile(cp, (1, 2)), jnp.tile(sp, (1, 2))])


def _trunk_prepared(x, layers, tables, c_sink, final_g):
    bn = x.shape[0]
    x = x.reshape(bn * SEQ, D_MODEL)
    for l, lw in enumerate(layers):
        a, b, c, d0, d1, d2 = _project(x, lw, tables)
        seq = lambda t: t.reshape(bn, -1, t.shape[-1])
        ys = (
            _latent_attention(seq(a)),
            _axial_attention(seq(b)),
            _window_attention(seq(c), c_sink[l]),
            _dilated_attention([seq(d0), seq(d1), seq(d2)]),
        )
        ys = [y.reshape(bn * SEQ, BRANCH_W) for y in ys]
        x = _merge(x, ys, lw, final_g, final=(l == DEPTH - 1))
    return x.reshape(bn, SEQ, D_MODEL)


def _prepare(norm_in, w_in, a_q_norm, w_q_up, a_kv_norm, w_kv_up, b_q_norm, b_k_norm, c_sink, w_branch, w_out, final_norm):
    layers = [_layer_weights(l, norm_in, w_in, a_q_norm, w_q_up, a_kv_norm, w_kv_up, b_q_norm, b_k_norm, w_branch, w_out)
              for l in range(DEPTH)]
    return layers, _rope_tables(), c_sink, final_norm.reshape(1, D_MODEL)


def _trunk(x, *params):
    return _trunk_prepared(x, *_prepare(*params))


def kernel(x_prompt, x_sample, norm_in, w_in, a_q_norm, w_q_up, a_kv_norm, w_kv_up, b_q_norm, b_k_norm, c_sink, w_branch, w_out, final_norm):
    prepared = _prepare(norm_in, w_in, a_q_norm, w_q_up, a_kv_norm, w_kv_up, b_q_norm, b_k_norm, c_sink, w_branch, w_out, final_norm)
    return (_trunk_prepared(x_prompt, *prepared), _trunk_prepared(x_sample, *prepared))
```
